```python
import jax, jax.numpy as jnp
from jax import lax
import numpy as np

D_MODEL = 1024
BATCH = 8
SEQ = 2048
DEPTH = 4

ATTN_Q_HEADS = 8
ATTN_KV_HEADS = 2
ATTN_HEAD_DIM = 64
ATTN_GROUP = ATTN_Q_HEADS // ATTN_KV_HEADS
WINDOW = 128
ATTN_BLOCK = 128
ROPE_THETA = 500000.0
ROPE_DIM = ATTN_HEAD_DIM // 4

RET_HEADS = 4
RET_QK_DIM = 64
RET_V_DIM = 128
RET_CHUNK = 128
RET_ROPE_THETA = 10000.0

CONV_CH = 512
CONV_WIDTH = 3

N_BRANCH = 3
BRANCH_W = 512

D_FF = ((8 * D_MODEL // 3 + 255) // 256) * 256

EPS = 1e-6

COL_SIZES = [
    ATTN_Q_HEADS * ATTN_HEAD_DIM,
    ATTN_KV_HEADS * ATTN_HEAD_DIM,
    ATTN_KV_HEADS * ATTN_HEAD_DIM,
    RET_HEADS * RET_QK_DIM,
    RET_HEADS * RET_QK_DIM,
    RET_HEADS * RET_V_DIM,
    RET_HEADS * RET_V_DIM,
    CONV_CH,
    CONV_CH,
    CONV_CH,
    N_BRANCH * D_MODEL,
]
D_IN = int(sum(COL_SIZES))
SPLIT_POINTS = [int(v) for v in np.cumsum(COL_SIZES)[:-1]]

kernel_name = "hybrid_swa_retention_shortconv_gated_block"


def rms_norm(x, g):
    x32 = x.astype(jnp.float32)
    y = x32 * lax.rsqrt(jnp.mean(x32 * x32, axis=-1, keepdims=True) + EPS)
    return (y * g.astype(jnp.float32)).astype(x.dtype)


def rotary(x, rot_dim, theta):
    seq = x.shape[1]
    half = rot_dim // 2
    inv = theta ** (-jnp.arange(half, dtype=jnp.float32) / half)
    ang = jnp.arange(seq, dtype=jnp.float32)[:, None] * inv[None, :]
    cos = jnp.cos(ang)[None, :, None, :]
    sin = jnp.sin(ang)[None, :, None, :]
    xr = x[..., :rot_dim].astype(jnp.float32)
    x1, x2 = xr[..., :half], xr[..., half:]
    rot = jnp.concatenate([x1 * cos - x2 * sin, x2 * cos + x1 * sin], axis=-1)
    return jnp.concatenate([rot.astype(x.dtype), x[..., rot_dim:]], axis=-1)


def swa_attention(q, k, v, sinks):
    b, s, _ = q.shape
    nb = s // ATTN_BLOCK
    dt = q.dtype
    q = rotary(q.reshape(b, s, ATTN_Q_HEADS, ATTN_HEAD_DIM), ROPE_DIM, ROPE_THETA)
    k = rotary(k.reshape(b, s, ATTN_KV_HEADS, ATTN_HEAD_DIM), ROPE_DIM, ROPE_THETA)
    v = v.reshape(b, s, ATTN_KV_HEADS, ATTN_HEAD_DIM)
    qb = (q.astype(jnp.float32) * ATTN_HEAD_DIM ** -0.5).reshape(
        b, nb, ATTN_BLOCK, ATTN_KV_HEADS, ATTN_GROUP, ATTN_HEAD_DIM)
    pad = ((0, 0), (ATTN_BLOCK, 0), (0, 0), (0, 0))
    kp = jnp.pad(k, pad).astype(jnp.float32).reshape(b, nb + 1, ATTN_BLOCK, ATTN_KV_HEADS, ATTN_HEAD_DIM)
    vp = jnp.pad(v, pad).astype(jnp.float32).reshape(b, nb + 1, ATTN_BLOCK, ATTN_KV_HEADS, ATTN_HEAD_DIM)
    kw = jnp.concatenate([kp[:, :-1], kp[:, 1:]], axis=2)
    vw = jnp.concatenate([vp[:, :-1], vp[:, 1:]], axis=2)
    qi = jnp.arange(ATTN_BLOCK)[:, None]
    kj = jnp.arange(2 * ATTN_BLOCK)[None, :]
    diff = ATTN_BLOCK + qi - kj
    n_idx = jnp.arange(nb)[:, None, None]
    kpos = (n_idx - 1) * ATTN_BLOCK + kj[None]
    mask = (diff[None] >= 0) & (diff[None] < WINDOW) & (kpos >= 0)
    scores = jnp.einsum('bnqhgd,bnkhd->bnhgqk', qb, kw)
    scores = jnp.where(mask[None, :, None, None], scores, -jnp.inf)
    sink = sinks.astype(jnp.float32).reshape(1, 1, ATTN_KV_HEADS, ATTN_GROUP, 1, 1)
    m = jnp.maximum(jnp.max(scores, axis=-1, keepdims=True), sink)
    p = jnp.exp(scores - m)
    denom = jnp.sum(p, axis=-1, keepdims=True) + jnp.exp(sink - m)
    p = p / denom
    out = jnp.einsum('bnhgqk,bnkhd->bnqhgd', p, vw)
    return out.reshape(b, s, ATTN_Q_HEADS * ATTN_HEAD_DIM).astype(dt)


def retention(q, k, v, g):
    b, s, _ = q.shape
    nc = s // RET_CHUNK
    dt = q.dtype
    q = rotary(q.reshape(b, s, RET_HEADS, RET_QK_DIM), RET_QK_DIM, RET_ROPE_THETA)
    k = rotary(k.reshape(b, s, RET_HEADS, RET_QK_DIM), RET_QK_DIM, RET_ROPE_THETA)
    qc = q.astype(jnp.float32).reshape(b, nc, RET_CHUNK, RET_HEADS, RET_QK_DIM)
    kc = (k.astype(jnp.float32) * RET_QK_DIM ** -0.5).reshape(b, nc, RET_CHUNK, RET_HEADS, RET_QK_DIM)
    vc = v.astype(jnp.float32).reshape(b, nc, RET_CHUNK, RET_HEADS, RET_V_DIM)
    log_gamma = jnp.log1p(-jnp.exp2(-(5.0 + jnp.arange(RET_HEADS, dtype=jnp.float32))))
    idx = jnp.arange(RET_CHUNK, dtype=jnp.float32)
    rel = idx[:, None] - idx[None, :]
    dmask = jnp.exp(jnp.where(rel[None] >= 0, log_gamma[:, None, None] * rel[None], -jnp.inf))
    att = jnp.einsum('bnqhd,bnkhd->bnhqk', qc, kc) * dmask[None, None]
    inner = jnp.einsum('bnhqk,bnkhe->bnqhe', att, vc)
    zeta = jnp.exp(log_gamma[:, None] * (RET_CHUNK - 1.0 - idx)[None])
    xi = jnp.exp(log_gamma[:, None] * (idx + 1.0)[None])
    chunk_decay = jnp.exp(log_gamma * RET_CHUNK)[None, :, None, None]
    kv = jnp.einsum('bnkhd,hk,bnkhe->nbhde', kc, zeta, vc)

    def step(state, kv_n):
        return chunk_decay * state + kv_n, state

    _, state_prev = lax.scan(step, jnp.zeros((b, RET_HEADS, RET_QK_DIM, RET_V_DIM), jnp.float32), kv)
    cross = jnp.einsum('bnqhd,hq,nbhde->bnqhe', qc, xi, state_prev)
    y = (inner + cross).reshape(b, s, RET_HEADS, RET_V_DIM)
    y = y * lax.rsqrt(jnp.mean(y * y, axis=-1, keepdims=True) + EPS)
    out = jax.nn.silu(g.reshape(b, s, RET_HEADS, RET_V_DIM).astype(jnp.float32)) * y
    return out.reshape(b, s, RET_HEADS * RET_V_DIM).astype(dt)


def short_conv(gate_b, gate_c, xv, w):
    u = gate_c * xv
    y = lax.conv_general_dilated(
        u, w.astype(u.dtype)[:, None, :], window_strides=(1,),
        padding=[(CONV_WIDTH - 1, 0)], dimension_numbers=('NWC', 'WIO', 'NWC'),
        feature_group_count=CONV_CH)
    return gate_b * y


def token_mixer(u, w_in_l, sinks_l, conv_w_l, w_branch_l, b_gate_l, w_out_l):
    b, s, _ = u.shape
    proj = u @ w_in_l
    (aq, ak, av, rq, rk, rv, rg, cb, cc, cx, gates) = jnp.split(proj, SPLIT_POINTS, axis=-1)
    ya = swa_attention(aq, ak, av, sinks_l)
    yr = retention(rq, rk, rv, rg)
    yc = short_conv(cb, cc, cx, conv_w_l)
    branches = jnp.stack([ya, yr, yc], axis=-2)
    branch_out = jnp.einsum('bsnc,ncd->bsnd', branches, w_branch_l)
    g = jax.nn.sigmoid(gates.reshape(b, s, N_BRANCH, D_MODEL) + b_gate_l)
    merged = jnp.sum(g * branch_out, axis=-2)
    return merged @ w_out_l


def swiglu(u, wg, wu, wd):
    return (jax.nn.silu(u @ wg) * (u @ wu)) @ wd


def setup_inputs(seed: int = 0) -> dict:
    key = jax.random.key(seed)
    ks = jax.random.split(key, 14)
    f32 = jnp.float32
    nrm = lambda k, shp, sc: jax.random.normal(k, shp, f32) * sc
    return {
        "x": nrm(ks[0], (BATCH, SEQ, D_MODEL), 1.0),
        "norm_mix": 1.0 + nrm(ks[1], (DEPTH, D_MODEL), 0.02),
        "w_in": nrm(ks[2], (DEPTH, D_MODEL, D_IN), D_MODEL ** -0.5),
        "attn_sinks": nrm(ks[3], (DEPTH, ATTN_Q_HEADS), 0.5),
        "conv_w": nrm(ks[4], (DEPTH, CONV_WIDTH, CONV_CH), CONV_WIDTH ** -0.5),
        "w_branch": nrm(ks[5], (DEPTH, N_BRANCH, BRANCH_W, D_MODEL), BRANCH_W ** -0.5),
        "b_gate": nrm(ks[6], (DEPTH, N_BRANCH, D_MODEL), 0.1),
        "w_out": nrm(ks[7], (DEPTH, D_MODEL, D_MODEL), D_MODEL ** -0.5),
        "norm_ffn": 1.0 + nrm(ks[8], (DEPTH, D_MODEL), 0.02),
        "w_ffn_gate": nrm(ks[9], (DEPTH, D_MODEL, D_FF), D_MODEL ** -0.5),
        "w_ffn_up": nrm(ks[10], (DEPTH, D_MODEL, D_FF), D_MODEL ** -0.5),
        "w_ffn_down": nrm(ks[11], (DEPTH, D_FF, D_MODEL), D_FF ** -0.5),
        "norm_final": 1.0 + nrm(ks[12], (D_MODEL,), 0.02),
    }


def reference(x, norm_mix, w_in, attn_sinks, conv_w, w_branch, b_gate, w_out,
              norm_ffn, w_ffn_gate, w_ffn_up, w_ffn_down, norm_final):
    h = x
    for layer in range(DEPTH):
        u = rms_norm(h, norm_mix[layer])
        h = h + token_mixer(u, w_in[layer], attn_sinks[layer], conv_w[layer],
                            w_branch[layer], b_gate[layer], w_out[layer])
        u = rms_norm(h, norm_ffn[layer])
        h = h + swiglu(u, w_ffn_gate[layer], w_ffn_up[layer], w_ffn_down[layer])
    return rms_norm(h, norm_final)
```

```python
import functools

import numpy as np
import jax
import jax.numpy as jnp
from jax import lax
from jax.experimental import pallas as pl
from jax.experimental.pallas import tpu as pltpu

F32 = jnp.float32
BF16 = jnp.bfloat16

D_MODEL = 1024
DEPTH = 4
ATTN_Q_HEADS = 8
ATTN_KV_HEADS = 2
ATTN_HEAD_DIM = 64
WINDOW = 128
ROPE_THETA = 500000.0
ROPE_DIM = ATTN_HEAD_DIM // 4
RET_HEADS = 4
RET_QK_DIM = 64
RET_V_DIM = 128
RET_ROPE_THETA = 10000.0
CONV_CH = 512
N_BRANCH = 3
BRANCH_W = 512
D_FF = 2816
EPS = 1e-6
CHUNK = 128

C_AQ, C_AK, C_AV = 0, 512, 640
C_RQ, C_RK, C_RV, C_RG = 768, 1024, 1280, 1792
C_CB, C_CC, C_CX = 2304, 2816, 3328
MIX_W = 3840
GATE_W = N_BRANCH * D_MODEL
D_IN = MIX_W + GATE_W

V7X_LANES = 128
V7X_VMEM_BYTES = 64 * 1024 * 1024
V7X_SCOPED_VMEM_CAP = 60000 * 1024

PROJ_TM = 512
PROJ_CHUNK = 768
MIX_T = 256
MERGE_TM = 512
FFN_TM = 512
FFN_CHUNKS = ((0, 1024), (1024, 1024), (2048, 768))

RET_CHUNK_DECAY = tuple(float(np.exp(np.log1p(-(2.0 ** -(5 + h))) * CHUNK)) for h in range(RET_HEADS))


def _vmem_limit(estimate_bytes):
    return int(min(V7X_SCOPED_VMEM_CAP, max(32 * 1024 * 1024, estimate_bytes * 5 // 4)))


def _rms(x, g):
    ms = jnp.mean(x * x, axis=-1, keepdims=True)
    return x * lax.rsqrt(ms + EPS) * g


def _resident(block_shape, index_map):
    return pl.BlockSpec(block_shape, index_map, pipeline_mode=pl.Buffered(1))


def _proj_kernel(x_ref, g_ref, w_ref, bg_ref, pm_ref, gate_ref):
    u = _rms(x_ref[...], g_ref[...]).astype(BF16)
    for c0 in range(0, MIX_W, PROJ_CHUNK):
        pm_ref[:, c0:c0 + PROJ_CHUNK] = jnp.dot(
            u, w_ref[:, c0:c0 + PROJ_CHUNK], preferred_element_type=F32).astype(BF16)
    for c0 in range(0, GATE_W, PROJ_CHUNK):
        z = jnp.dot(u, w_ref[:, MIX_W + c0:MIX_W + c0 + PROJ_CHUNK], preferred_element_type=F32)
        gate_ref[:, c0:c0 + PROJ_CHUNK] = jax.nn.sigmoid(z + bg_ref[:, c0:c0 + PROJ_CHUNK]).astype(BF16)


def _proj(h, norm_g, w_in, b_gate, layer):
    m = h.shape[0]
    tm = PROJ_TM
    est = (D_MODEL * D_IN * 2 + 2 * tm * D_MODEL * 4 + 2 * tm * D_IN * 2 + 2 * tm * PROJ_CHUNK * 4)
    return pl.pallas_call(
        _proj_kernel,
        grid=(m // tm,),
        in_specs=[
            pl.BlockSpec((tm, D_MODEL), lambda i: (i, 0)),
            _resident((None, 1, D_MODEL), lambda i: (layer, 0, 0)),
            _resident((None, D_MODEL, D_IN), lambda i: (layer, 0, 0)),
            _resident((None, 1, GATE_W), lambda i: (layer, 0, 0)),
        ],
        out_specs=[
            pl.BlockSpec((tm, MIX_W), lambda i: (i, 0)),
            pl.BlockSpec((tm, GATE_W), lambda i: (i, 0)),
        ],
        out_shape=[
            jax.ShapeDtypeStruct((m, MIX_W), BF16),
            jax.ShapeDtypeStruct((m, GATE_W), BF16),
        ],
        compiler_params=pltpu.CompilerParams(
            dimension_semantics=("arbitrary",), vmem_limit_bytes=_vmem_limit(est)),
        name="proj",
    )(h, norm_g, w_in, b_gate)


def _rot128(z, cos, coef_next, coef_prev, shift):
    return (z * cos + pltpu.roll(z, V7X_LANES - shift, 1) * coef_next
            + pltpu.roll(z, shift, 1) * coef_prev)


def _mixer_kernel(sinks_ref, pm_ref, rot_ref, dmask_ref, xz_ref, convw_ref, br_ref,
                  kx_ref, vx_ref, state_ref, ubuf_ref, *, layer):
    t = pl.program_id(1)
    T = MIX_T
    nch = T // CHUNK
    half = ATTN_HEAD_DIM

    @pl.when(t == 0)
    def _():
        kx_ref[:, 0:CHUNK, :] = jnp.zeros((4, CHUNK, V7X_LANES), BF16)
        vx_ref[:, 0:CHUNK, :] = jnp.zeros((4, CHUNK, V7X_LANES), BF16)
        state_ref[...] = jnp.zeros_like(state_ref)
        ubuf_ref[0:8, :] = jnp.zeros((8, CONV_CH), F32)

    lo_t = lax.broadcasted_iota(jnp.int32, (T, V7X_LANES), 1) < half

    acos, anext, aprev = rot_ref[:, 0:128], rot_ref[:, 128:256], rot_ref[:, 256:384]
    scale = ATTN_HEAD_DIM ** -0.5
    qb = []
    for g in range(4):
        zg = pm_ref[:, C_AQ + g * 128:C_AQ + (g + 1) * 128].astype(F32)
        qb.append((_rot128(zg, acos, anext, aprev, ROPE_DIM // 2) * scale).astype(BF16))
    k = _rot128(pm_ref[:, C_AK:C_AK + 128].astype(F32), acos, anext, aprev, ROPE_DIM // 2)
    v = pm_ref[:, C_AV:C_AV + 128].astype(F32)
    for ref, val in ((kx_ref, k), (vx_ref, v)):
        swapped = pltpu.roll(val, half, 1)
        ref[0, CHUNK:CHUNK + T, :] = jnp.where(lo_t, val, 0.0).astype(BF16)
        ref[1, CHUNK:CHUNK + T, :] = jnp.where(lo_t, 0.0, swapped).astype(BF16)
        ref[2, CHUNK:CHUNK + T, :] = jnp.where(lo_t, swapped, 0.0).astype(BF16)
        ref[3, CHUNK:CHUNK + T, :] = jnp.where(lo_t, 0.0, val).astype(BF16)

    row2 = lax.broadcasted_iota(jnp.int32, (2 * CHUNK, 2 * CHUNK), 0) & (CHUNK - 1)
    col2 = lax.broadcasted_iota(jnp.int32, (2 * CHUNK, 2 * CHUNK), 1)
    rel = col2 - row2
    band = (rel >= 1) & (rel <= WINDOW)
    first_lim = jnp.where(t == 0, CHUNK, 0)
    band_first = band & (col2 >= first_lim)
    top_rows = lax.broadcasted_iota(jnp.int32, (2 * CHUNK, 1), 0) < CHUNK
    lo_2c = lax.broadcasted_iota(jnp.int32, (2 * CHUNK, V7X_LANES), 1) < half

    for c in range(nch):
        r0 = c * CHUNK
        mask = band_first if c == 0 else band
        for h in range(ATTN_KV_HEADS):
            lhs = jnp.concatenate([qb[2 * h][r0:r0 + CHUNK], qb[2 * h + 1][r0:r0 + CHUNK]], axis=0)
            outs = []
            for ab in range(2):
                kw = kx_ref[2 * h + ab, r0:r0 + 2 * CHUNK, :]
                vw = vx_ref[2 * h + ab, r0:r0 + 2 * CHUNK, :]
                s = lax.dot_general(lhs, kw, (((1,), (1,)), ((), ())), preferred_element_type=F32)
                s = jnp.where(mask, s, -jnp.inf)
                sink = jnp.where(top_rows, sinks_ref[layer, 4 * h + ab], sinks_ref[layer, 4 * h + 2 + ab])
                mx = jnp.maximum(jnp.max(s, axis=-1, keepdims=True), sink)
                p = jnp.exp(s - mx)
                den = jnp.sum(p, axis=-1, keepdims=True) + jnp.exp(sink - mx)
                o = jnp.dot(p.astype(BF16), vw, preferred_element_type=F32)
                outs.append(o * (1.0 / den))
            out = jnp.where(lo_2c, outs[0], outs[1]).astype(BF16)
            br_ref[r0:r0 + CHUNK, (2 * h) * 128:(2 * h + 1) * 128] = out[0:CHUNK]
            br_ref[r0:r0 + CHUNK, (2 * h + 1) * 128:(2 * h + 2) * 128] = out[CHUNK:2 * CHUNK]

    kx_ref[:, 0:CHUNK, :] = kx_ref[:, T:T + CHUNK, :]
    vx_ref[:, 0:CHUNK, :] = vx_ref[:, T:T + CHUNK, :]

    rcos, rnext, rprev = rot_ref[:, 384:512], rot_ref[:, 512:640], rot_ref[:, 640:768]
    lo_c = lax.broadcasted_iota(jnp.int32, (CHUNK, V7X_LANES), 1) < half
    top_c = lax.broadcasted_iota(jnp.int32, (CHUNK, V7X_LANES), 0) < half
    kscale = RET_QK_DIM ** -0.5
    for pp in range(RET_HEADS // 2):
        qp = _rot128(pm_ref[:, C_RQ + pp * 128:C_RQ + (pp + 1) * 128].astype(F32),
                     rcos, rnext, rprev, RET_QK_DIM // 2)
        kp = _rot128(pm_ref[:, C_RK + pp * 128:C_RK + (pp + 1) * 128].astype(F32),
                     rcos, rnext, rprev, RET_QK_DIM // 2) * kscale
        xi = xz_ref[:, pp * 128:(pp + 1) * 128]
        zeta = xz_ref[:, 256 + pp * 128:256 + (pp + 1) * 128]
        for c in range(nch):
            r0 = c * CHUNK
            qc = qp[r0:r0 + CHUNK]
            kc = kp[r0:r0 + CHUNK]
            kbd = jnp.concatenate([jnp.where(lo_c, kc, 0.0), jnp.where(lo_c, 0.0, kc)], axis=0).astype(BF16)
            s2 = lax.dot_general(qc.astype(BF16), kbd, (((1,), (1,)), ((), ())),
                                 preferred_element_type=F32)
            qxi = (qc * xi).astype(BF16)
            kz = (kc * zeta).astype(BF16)
            for hh in range(2):
                h = 2 * pp + hh
                vh = pm_ref[r0:r0 + CHUNK, C_RV + h * 128:C_RV + (h + 1) * 128]
                gh = pm_ref[r0:r0 + CHUNK, C_RG + h * 128:C_RG + (h + 1) * 128].astype(F32)
                att = (s2[:, hh * CHUNK:(hh + 1) * CHUNK] * dmask_ref[h]).astype(BF16)
                st = state_ref[h]
                y = jnp.dot(jnp.concatenate([att, qxi], axis=1),
                            jnp.concatenate([vh, st.astype(BF16)], axis=0),
                            preferred_element_type=F32)
                kv = lax.dot_general(kz, vh, (((0,), (0,)), ((), ())), preferred_element_type=F32)
                own_rows = top_c if hh == 0 else jnp.logical_not(top_c)
                state_ref[h] = RET_CHUNK_DECAY[h] * st + jnp.where(own_rows, kv, 0.0)
                yn = y * lax.rsqrt(jnp.mean(y * y, axis=-1, keepdims=True) + EPS)
                out = gh * jax.nn.sigmoid(gh) * yn
                br_ref[r0:r0 + CHUNK, BRANCH_W + h * 128:BRANCH_W + (h + 1) * 128] = out.astype(BF16)

    cb = pm_ref[:, C_CB:C_CB + CONV_CH].astype(F32)
    cc = pm_ref[:, C_CC:C_CC + CONV_CH].astype(F32)
    cx = pm_ref[:, C_CX:C_CX + CONV_CH].astype(F32)
    u0 = cc * cx
    ubuf_ref[8:8 + T, :] = u0
    u1 = ubuf_ref[7:7 + T, :]
    u2 = ubuf_ref[6:6 + T, :]
    w = convw_ref[...]
    yc = cb * (w[0:1, :] * u2 + w[1:2, :] * u1 + w[2:3, :] * u0)
    br_ref[:, 2 * BRANCH_W:3 * BRANCH_W] = yc.astype(BF16)
    ubuf_ref[0:8, :] = ubuf_ref[T:T + 8, :]


def _mixer(pm, sinks, rot_tab, dmask, xz, conv_w, batch, seq, layer):
    T = MIX_T
    nt = seq // T
    m = batch * seq
    est = (2 * T * MIX_W * 2 + 2 * T * 768 * 4 + 2 * 4 * CHUNK * CHUNK * 4 + 2 * CHUNK * 512 * 4
           + 2 * T * 3 * BRANCH_W * 2 + 2 * 4 * (CHUNK + T) * 128 * 2 + 4 * CHUNK * CHUNK * 4
           + (8 + T) * CONV_CH * 4 + 16 * T * 512 * 4)
    return pl.pallas_call(
        functools.partial(_mixer_kernel, layer=layer),
        grid=(batch, nt),
        in_specs=[
            pl.BlockSpec(memory_space=pltpu.SMEM),
            pl.BlockSpec((T, MIX_W), lambda b, t: (b * nt + t, 0)),
            pl.BlockSpec((T, 768), lambda b, t: (t, 0)),
            pl.BlockSpec((RET_HEADS, CHUNK, CHUNK), lambda b, t: (0, 0, 0)),
            pl.BlockSpec((CHUNK, 512), lambda b, t: (0, 0)),
            pl.BlockSpec((None, 3, CONV_CH), lambda b, t: (layer, 0, 0)),
        ],
        out_specs=pl.BlockSpec((T, 3 * BRANCH_W), lambda b, t: (b * nt + t, 0)),
        out_shape=jax.ShapeDtypeStruct((m, 3 * BRANCH_W), BF16),
        scratch_shapes=[
            pltpu.VMEM((4, CHUNK + T, V7X_LANES), BF16),
            pltpu.VMEM((4, CHUNK + T, V7X_LANES), BF16),
            pltpu.VMEM((RET_HEADS, CHUNK, CHUNK), F32),
            pltpu.VMEM((8 + T, CONV_CH), F32),
        ],
        compiler_params=pltpu.CompilerParams(
            dimension_semantics=("arbitrary", "arbitrary"), vmem_limit_bytes=_vmem_limit(est)),
        name="mixer",
    )(sinks, pm, rot_tab, dmask, xz, conv_w)


def _merge_kernel(x_ref, br_ref, gate_ref, wb_ref, wo_ref, o_ref):
    acc = None
    for i in range(N_BRANCH):
        y = jnp.dot(br_ref[:, i * BRANCH_W:(i + 1) * BRANCH_W], wb_ref[i], preferred_element_type=F32)
        term = gate_ref[:, i * D_MODEL:(i + 1) * D_MODEL].astype(F32) * y
        acc = term if acc is None else acc + term
    o_ref[...] = x_ref[...] + jnp.dot(acc.astype(BF16), wo_ref[...], preferred_element_type=F32)


def _merge(h, br, gate, w_branch, w_out, layer):
    m = h.shape[0]
    tm = MERGE_TM
    est = (N_BRANCH * BRANCH_W * D_MODEL * 2 + D_MODEL * D_MODEL * 2 + 4 * tm * D_MODEL * 4
           + 2 * tm * 3 * BRANCH_W * 2 + 2 * tm * GATE_W * 2 + 3 * tm * D_MODEL * 4)
    return pl.pallas_call(
        _merge_kernel,
        grid=(m // tm,),
        in_specs=[
            pl.BlockSpec((tm, D_MODEL), lambda i: (i, 0)),
            pl.BlockSpec((tm, 3 * BRANCH_W), lambda i: (i, 0)),
            pl.BlockSpec((tm, GATE_W), lambda i: (i, 0)),
            _resident((None, N_BRANCH, BRANCH_W, D_MODEL), lambda i: (layer, 0, 0, 0)),
            _resident((None, D_MODEL, D_MODEL), lambda i: (layer, 0, 0)),
        ],
        out_specs=pl.BlockSpec((tm, D_MODEL), lambda i: (i, 0)),
        out_shape=jax.ShapeDtypeStruct((m, D_MODEL), F32),
        compiler_params=pltpu.CompilerParams(
            dimension_semantics=("arbitrary",), vmem_limit_bytes=_vmem_limit(est)),
        name="merge",
    )(h, br, gate, w_branch, w_out)


def _ffn_kernel(x_ref, g_ref, wg_ref, wu_ref, wd_ref, gf_ref, o_ref, hid_ref, *, final):
    x = x_ref[...]
    u = _rms(x, g_ref[...]).astype(BF16)
    for c0, cw in FFN_CHUNKS:
        a = jnp.dot(u, wg_ref[:, c0:c0 + cw], preferred_element_type=F32)
        b = jnp.dot(u, wu_ref[:, c0:c0 + cw], preferred_element_type=F32)
        hid_ref[:, c0:c0 + cw] = (a * jax.nn.sigmoid(a) * b).astype(BF16)
    out = x + jnp.dot(hid_ref[...], wd_ref[...], preferred_element_type=F32)
    if final:
        out = _rms(out, gf_ref[...])
    o_ref[...] = out


def _ffn(h, norm_g, w_gate, w_up, w_down, norm_final, layer, final):
    m = h.shape[0]
    tm = FFN_TM
    est = (3 * D_MODEL * D_FF * 2 + 4 * tm * D_MODEL * 4 + tm * D_FF * 2 + 4 * tm * 1024 * 4)
    return pl.pallas_call(
        functools.partial(_ffn_kernel, final=final),
        grid=(m // tm,),
        in_specs=[
            pl.BlockSpec((tm, D_MODEL), lambda i: (i, 0)),
            _resident((None, 1, D_MODEL), lambda i: (layer, 0, 0)),
            _resident((None, D_MODEL, D_FF), lambda i: (layer, 0, 0)),
            _resident((None, D_MODEL, D_FF), lambda i: (layer, 0, 0)),
            _resident((None, D_FF, D_MODEL), lambda i: (layer, 0, 0)),
            _resident((1, D_MODEL), lambda i: (0, 0)),
        ],
        out_specs=pl.BlockSpec((tm, D_MODEL), lambda i: (i, 0)),
        out_shape=jax.ShapeDtypeStruct((m, D_MODEL), F32),
        scratch_shapes=[pltpu.VMEM((tm, D_FF), BF16)],
        compiler_params=pltpu.CompilerParams(
            dimension_semantics=("arbitrary",), vmem_limit_bytes=_vmem_limit(est)),
        name="ffn",
    )(h, norm_g, w_gate, w_up, w_down, norm_final)


def _rotary_table(seq, rot_dim, head_dim, theta):
    half = rot_dim // 2
    inv = theta ** (-jnp.arange(half, dtype=F32) / half)
    ang = jnp.arange(seq, dtype=F32)[:, None] * inv[None, :]
    cos, sin = jnp.cos(ang), jnp.sin(ang)
    zeros = jnp.zeros((seq, half), F32)
    tail0 = jnp.zeros((seq, head_dim - rot_dim), F32)
    tail1 = jnp.ones((seq, head_dim - rot_dim), F32)
    c = jnp.concatenate([cos, cos, tail1], axis=1)
    nxt = jnp.concatenate([-sin, zeros, tail0], axis=1)
    prv = jnp.concatenate([zeros, sin, tail0], axis=1)
    return jnp.concatenate([c, c, nxt, nxt, prv, prv], axis=1)


def _retention_tables():
    log_gamma = jnp.log1p(-jnp.exp2(-(5.0 + jnp.arange(RET_HEADS, dtype=F32))))
    idx = jnp.arange(CHUNK, dtype=F32)
    rel = idx[:, None] - idx[None, :]
    dmask = jnp.exp(jnp.where(rel[None] >= 0, log_gamma[:, None, None] * rel[None], -jnp.inf))
    zeta = jnp.exp(log_gamma[:, None] * (CHUNK - 1.0 - idx)[None])
    xi = jnp.exp(log_gamma[:, None] * (idx + 1.0)[None])
    widen = lambda a: jnp.repeat(a.T, RET_QK_DIM, axis=1)
    return dmask, jnp.concatenate([widen(xi), widen(zeta)], axis=1)


def kernel(x, norm_mix, w_in, attn_sinks, conv_w, w_branch, b_gate, w_out,
           norm_ffn, w_ffn_gate, w_ffn_up, w_ffn_down, norm_final):
    batch, seq, d = x.shape
    assert d == D_MODEL and seq % MIX_T == 0 and (batch * seq) % PROJ_TM == 0
    assert w_in.shape == (DEPTH, D_MODEL, D_IN)

    rot_tab = jnp.concatenate([
        _rotary_table(seq, ROPE_DIM, ATTN_HEAD_DIM, ROPE_THETA),
        _rotary_table(seq, RET_QK_DIM, RET_QK_DIM, RET_ROPE_THETA)], axis=1)
    dmask, xz = _retention_tables()

    w_in_b = w_in.astype(BF16)
    w_branch_b = w_branch.astype(BF16)
    w_out_b = w_out.astype(BF16)
    w_gate_b = w_ffn_gate.astype(BF16)
    w_up_b = w_ffn_up.astype(BF16)
    w_down_b = w_ffn_down.astype(BF16)
    norm_mix3 = norm_mix.reshape(DEPTH, 1, D_MODEL)
    norm_ffn3 = norm_ffn.reshape(DEPTH, 1, D_MODEL)
    b_gate3 = b_gate.reshape(DEPTH, 1, GATE_W)
    norm_final2 = norm_final.reshape(1, D_MODEL)

    h = x.reshape(batch * seq, D_MODEL)
    for layer in range(DEPTH):
        pm, gate = _proj(h, norm_mix3, w_in_b, b_gate3, layer)
        br = _mixer(pm, attn_sinks, rot_tab, dmask, xz, conv_w, batch, seq, layer)
        h = _merge(h, br, gate, w_branch_b, w_out_b, layer)
        h = _ffn(h, norm_ffn3, w_gate_b, w_up_b, w_down_b, norm_final2, layer, layer == DEPTH - 1)
    return h.reshape(batch, seq, D_MODEL)
```

```python
import functools

import numpy as np
import jax
import jax.numpy as jnp
from jax import lax
from jax.experimental import pallas as pl
from jax.experimental.pallas import tpu as pltpu

F32 = jnp.float32
BF16 = jnp.bfloat16

D_MODEL = 1024
DEPTH = 4
ATTN_Q_HEADS = 8
ATTN_KV_HEADS = 2
ATTN_HEAD_DIM = 64
WINDOW = 128
ROPE_THETA = 500000.0
ROPE_DIM = ATTN_HEAD_DIM // 4
RET_HEADS = 4
RET_QK_DIM = 64
RET_V_DIM = 128
RET_ROPE_THETA = 10000.0
CONV_CH = 512
N_BRANCH = 3
BRANCH_W = 512
D_FF = 2816
EPS = 1e-6
CHUNK = 128

C_AQ, C_AK, C_AV = 0, 512, 640
C_RQ, C_RK, C_RV, C_RG = 768, 1024, 1280, 1792
C_CB, C_CC, C_CX = 2304, 2816, 3328
MIX_W = 3840
GATE_W = N_BRANCH * D_MODEL
D_IN = MIX_W + GATE_W

V7X_LANES = 128
V7X_VMEM_BYTES = 64 * 1024 * 1024
V7X_SCOPED_VMEM_CAP = 60000 * 1024

MIX_T = 512
GATE_CHUNK = 512
MERGE_TM = 512
FFN_TM = 512
FFN_CHUNKS = ((0, 1024), (1024, 1024), (2048, 768))

RET_CHUNK_DECAY = tuple(float(np.exp(np.log1p(-(2.0 ** -(5 + h))) * CHUNK)) for h in range(RET_HEADS))


def _vmem_limit(estimate_bytes):
    return int(min(V7X_SCOPED_VMEM_CAP, max(32 * 1024 * 1024, estimate_bytes * 5 // 4)))


def _rms(x, g):
    ms = jnp.mean(x * x, axis=-1, keepdims=True)
    return x * lax.rsqrt(ms + EPS) * g


def _resident(block_shape, index_map):
    return pl.BlockSpec(block_shape, index_map, pipeline_mode=pl.Buffered(1))


def _rot128(z, cos, coef_next, coef_prev, shift):
    return (z * cos + pltpu.roll(z, V7X_LANES - shift, 1) * coef_next
            + pltpu.roll(z, shift, 1) * coef_prev)


class _Attention:
    def __init__(self, p_attn, rot_ref, sinks_ref, br_ref, kx_ref, vx_ref, t, layer):
        T = MIX_T
        half = ATTN_HEAD_DIM
        self.sinks_ref, self.br_ref, self.kx_ref, self.vx_ref, self.layer = sinks_ref, br_ref, kx_ref, vx_ref, layer
        lo_t = lax.broadcasted_iota(jnp.int32, (T, V7X_LANES), 1) < half
        acos, anext, aprev = rot_ref[:, 0:128], rot_ref[:, 128:256], rot_ref[:, 256:384]
        scale = ATTN_HEAD_DIM ** -0.5
        self.qb = []
        for g in range(4):
            zg = p_attn[:, C_AQ + g * 128:C_AQ + (g + 1) * 128]
            self.qb.append((_rot128(zg, acos, anext, aprev, ROPE_DIM // 2) * scale).astype(BF16))
        k = _rot128(p_attn[:, C_AK:C_AK + 128], acos, anext, aprev, ROPE_DIM // 2)
        v = p_attn[:, C_AV:C_AV + 128]
        for ref, val in ((kx_ref, k), (vx_ref, v)):
            swapped = pltpu.roll(val, half, 1)
            ref[0, CHUNK:CHUNK + T, :] = jnp.where(lo_t, val, 0.0).astype(BF16)
            ref[1, CHUNK:CHUNK + T, :] = jnp.where(lo_t, 0.0, swapped).astype(BF16)
            ref[2, CHUNK:CHUNK + T, :] = jnp.where(lo_t, swapped, 0.0).astype(BF16)
            ref[3, CHUNK:CHUNK + T, :] = jnp.where(lo_t, 0.0, val).astype(BF16)

        row2 = lax.broadcasted_iota(jnp.int32, (2 * CHUNK, 2 * CHUNK), 0) & (CHUNK - 1)
        col2 = lax.broadcasted_iota(jnp.int32, (2 * CHUNK, 2 * CHUNK), 1)
        rel = col2 - row2
        self.band = (rel >= 1) & (rel <= WINDOW)
        first_lim = jnp.where(t == 0, CHUNK, 0)
        self.band_first = self.band & (col2 >= first_lim)
        self.top_rows = lax.broadcasted_iota(jnp.int32, (2 * CHUNK, 1), 0) < CHUNK
        self.lo_2c = lax.broadcasted_iota(jnp.int32, (2 * CHUNK, V7X_LANES), 1) < half

    def scores(self, c):
        r0 = c * CHUNK
        out = []
        for h in range(ATTN_KV_HEADS):
            lhs = jnp.concatenate([self.qb[2 * h][r0:r0 + CHUNK], self.qb[2 * h + 1][r0:r0 + CHUNK]], axis=0)
            for ab in range(2):
                kw = self.kx_ref[2 * h + ab, r0:r0 + 2 * CHUNK, :]
                out.append(lax.dot_general(lhs, kw, (((1,), (1,)), ((), ())), preferred_element_type=F32))
        return out

    def finish(self, c, scores):
        r0 = c * CHUNK
        mask = self.band_first if c == 0 else self.band
        for h in range(ATTN_KV_HEADS):
            outs = []
            for ab in range(2):
                vw = self.vx_ref[2 * h + ab, r0:r0 + 2 * CHUNK, :]
                s = jnp.where(mask, scores[2 * h + ab], -jnp.inf)
                sink = jnp.where(self.top_rows, self.sinks_ref[self.layer, 4 * h + ab],
                                 self.sinks_ref[self.layer, 4 * h + 2 + ab])
                mx = jnp.maximum(jnp.max(s, axis=-1, keepdims=True), sink)
                p = jnp.exp(s - mx)
                den = jnp.sum(p, axis=-1, keepdims=True) + jnp.exp(sink - mx)
                o = jnp.dot(p.astype(BF16), vw, preferred_element_type=F32)
                outs.append(o * (1.0 / den))
            out = jnp.where(self.lo_2c, outs[0], outs[1]).astype(BF16)
            self.br_ref[r0:r0 + CHUNK, (2 * h) * 128:(2 * h + 1) * 128] = out[0:CHUNK]
            self.br_ref[r0:r0 + CHUNK, (2 * h + 1) * 128:(2 * h + 2) * 128] = out[CHUNK:2 * CHUNK]

    def carry(self):
        T = MIX_T
        self.kx_ref[:, 0:CHUNK, :] = self.kx_ref[:, T:T + CHUNK, :]
        self.vx_ref[:, 0:CHUNK, :] = self.vx_ref[:, T:T + CHUNK, :]


class _Retention:
    def __init__(self, p_qk, rot_ref, dmask_ref, xz_ref, br_ref, state_ref):
        half = RET_QK_DIM
        self.dmask_ref, self.xz_ref, self.br_ref, self.state_ref = dmask_ref, xz_ref, br_ref, state_ref
        rcos, rnext, rprev = rot_ref[:, 384:512], rot_ref[:, 512:640], rot_ref[:, 640:768]
        self.lo_c = lax.broadcasted_iota(jnp.int32, (CHUNK, V7X_LANES), 1) < half
        self.top_c = lax.broadcasted_iota(jnp.int32, (CHUNK, V7X_LANES), 0) < half
        kscale = RET_QK_DIM ** -0.5
        self.qp, self.kp = [], []
        for pp in range(RET_HEADS // 2):
            self.qp.append(_rot128(p_qk[:, pp * 128:(pp + 1) * 128], rcos, rnext, rprev, RET_QK_DIM // 2))
            self.kp.append(_rot128(p_qk[:, 256 + pp * 128:256 + (pp + 1) * 128], rcos, rnext, rprev,
                                   RET_QK_DIM // 2) * kscale)

    def scores(self, c):
        r0 = c * CHUNK
        out = []
        for pp in range(RET_HEADS // 2):
            qc = self.qp[pp][r0:r0 + CHUNK]
            kc = self.kp[pp][r0:r0 + CHUNK]
            kbd = jnp.concatenate([jnp.where(self.lo_c, kc, 0.0), jnp.where(self.lo_c, 0.0, kc)],
                                  axis=0).astype(BF16)
            out.append(lax.dot_general(qc.astype(BF16), kbd, (((1,), (1,)), ((), ())),
                                       preferred_element_type=F32))
        return out

    def finish(self, c, scores, p_v, p_g):
        r0 = c * CHUNK
        for pp in range(RET_HEADS // 2):
            qc = self.qp[pp][r0:r0 + CHUNK]
            kc = self.kp[pp][r0:r0 + CHUNK]
            qxi = (qc * self.xz_ref[:, pp * 128:(pp + 1) * 128]).astype(BF16)
            kz = (kc * self.xz_ref[:, 256 + pp * 128:256 + (pp + 1) * 128]).astype(BF16)
            for hh in range(2):
                h = 2 * pp + hh
                vh = p_v[r0:r0 + CHUNK, h * 128:(h + 1) * 128].astype(BF16)
                gh = p_g[r0:r0 + CHUNK, h * 128:(h + 1) * 128]
                att = (scores[pp][:, hh * CHUNK:(hh + 1) * CHUNK] * self.dmask_ref[h]).astype(BF16)
                st = self.state_ref[h]
                y = jnp.dot(jnp.concatenate([att, qxi], axis=1),
                            jnp.concatenate([vh, st.astype(BF16)], axis=0),
                            preferred_element_type=F32)
                kv = lax.dot_general(kz, vh, (((0,), (0,)), ((), ())), preferred_element_type=F32)
                own_rows = self.top_c if hh == 0 else jnp.logical_not(self.top_c)
                self.state_ref[h] = RET_CHUNK_DECAY[h] * st + jnp.where(own_rows, kv, 0.0)
                yn = y * lax.rsqrt(jnp.mean(y * y, axis=-1, keepdims=True) + EPS)
                out = gh * jax.nn.sigmoid(gh) * yn
                self.br_ref[r0:r0 + CHUNK, BRANCH_W + h * 128:BRANCH_W + (h + 1) * 128] = out.astype(BF16)


def _short_conv(cb, cc, cx, convw_ref, br_ref, ubuf_ref):
    T = MIX_T
    u0 = cc * cx
    ubuf_ref[8:8 + T, :] = u0
    u1 = ubuf_ref[7:7 + T, :]
    u2 = ubuf_ref[6:6 + T, :]
    w = convw_ref[...]
    yc = cb * (w[0:1, :] * u2 + w[1:2, :] * u1 + w[2:3, :] * u0)
    br_ref[:, 2 * BRANCH_W:3 * BRANCH_W] = yc.astype(BF16)
    ubuf_ref[0:8, :] = ubuf_ref[T:T + 8, :]


def _projmix_kernel(sinks_ref, x_ref, g_ref, w_ref, bg_ref, rot_ref, dmask_ref, xz_ref, convw_ref,
                    br_ref, gate_ref, kx_ref, vx_ref, state_ref, ubuf_ref, *, layer):
    t = pl.program_id(1)

    @pl.when(t == 0)
    def _():
        kx_ref[:, 0:CHUNK, :] = jnp.zeros((4, CHUNK, V7X_LANES), BF16)
        vx_ref[:, 0:CHUNK, :] = jnp.zeros((4, CHUNK, V7X_LANES), BF16)
        state_ref[...] = jnp.zeros_like(state_ref)
        ubuf_ref[0:8, :] = jnp.zeros((8, CONV_CH), F32)

    u = _rms(x_ref[...], g_ref[...]).astype(BF16)
    proj = lambda c0, c1: jnp.dot(u, w_ref[:, c0:c1], preferred_element_type=F32)
    nch = MIX_T // CHUNK

    attn = _Attention(proj(C_AQ, C_RQ), rot_ref, sinks_ref, br_ref, kx_ref, vx_ref, t, layer)
    ret = _Retention(proj(C_RQ, C_RV), rot_ref, dmask_ref, xz_ref, br_ref, state_ref)
    vals = {"rv": proj(C_RV, C_RG)}

    def gate_slice(c0):
        def emit():
            z = proj(MIX_W + c0, MIX_W + c0 + GATE_CHUNK)
            gate_ref[:, c0:c0 + GATE_CHUNK] = jax.nn.sigmoid(z + bg_ref[:, c0:c0 + GATE_CHUNK]).astype(BF16)
        return emit

    def mix_slice(name, c0, c1):
        def emit():
            vals[name] = proj(c0, c1)
            if name == "cx":
                _short_conv(vals["cb"], vals["cc"], vals["cx"], convw_ref, br_ref, ubuf_ref)
        return emit

    slices = [mix_slice("rg", C_RG, C_CB), mix_slice("cb", C_CB, C_CC), mix_slice("cc", C_CC, C_CX),
              mix_slice("cx", C_CX, MIX_W)] + [gate_slice(c0) for c0 in range(0, GATE_W, GATE_CHUNK)]
    units = [("a", 0), ("a", 1), ("r", 0), ("a", 2), ("r", 1), ("a", 3), ("r", 2), ("r", 3)]
    assert len(units) == 2 * nch
    n_slices = len(slices)

    def stage1(kind, c):
        return attn.scores(c) if kind == "a" else ret.scores(c)

    def stage2(kind, c, s):
        if kind == "a":
            attn.finish(c, s)
        else:
            ret.finish(c, s, vals["rv"], vals["rg"])

    pending = stage1(*units[0])
    for i, unit in enumerate(units):
        nxt = stage1(*units[i + 1]) if i + 1 < len(units) else None
        for _ in range(n_slices * (i + 1) // len(units) - n_slices * i // len(units)):
            slices.pop(0)()
        stage2(*unit, pending)
        pending = nxt
    assert not slices
    attn.carry()


def _projmix(h, sinks, norm_g, w_in, b_gate, rot_tab, dmask, xz, conv_w, batch, seq, layer):
    T = MIX_T
    nt = seq // T
    m = batch * seq
    est = (D_MODEL * D_IN * 2 + 2 * T * D_MODEL * 4 + 2 * T * 768 * 4 + 2 * 4 * CHUNK * CHUNK * 4
           + 2 * CHUNK * 512 * 4 + 2 * T * 3 * BRANCH_W * 2 + 2 * T * GATE_W * 2
           + 2 * 4 * (CHUNK + T) * 128 * 2 + 4 * CHUNK * CHUNK * 4 + (8 + T) * CONV_CH * 4
           + T * MIX_W * 4 + 8 * T * 512 * 4)
    return pl.pallas_call(
        functools.partial(_projmix_kernel, layer=layer),
        grid=(batch, nt),
        in_specs=[
            pl.BlockSpec(memory_space=pltpu.SMEM),
            pl.BlockSpec((T, D_MODEL), lambda b, t: (b * nt + t, 0)),
            _resident((None, 1, D_MODEL), lambda b, t: (layer, 0, 0)),
            _resident((None, D_MODEL, D_IN), lambda b, t: (layer, 0, 0)),
            _resident((None, 1, GATE_W), lambda b, t: (layer, 0, 0)),
            pl.BlockSpec((T, 768), lambda b, t: (t, 0)),
            _resident((RET_HEADS, CHUNK, CHUNK), lambda b, t: (0, 0, 0)),
            _resident((CHUNK, 512), lambda b, t: (0, 0)),
            _resident((None, 3, CONV_CH), lambda b, t: (layer, 0, 0)),
        ],
        out_specs=[
            pl.BlockSpec((T, 3 * BRANCH_W), lambda b, t: (b * nt + t, 0)),
            pl.BlockSpec((T, GATE_W), lambda b, t: (b * nt + t, 0)),
        ],
        out_shape=[
            jax.ShapeDtypeStruct((m, 3 * BRANCH_W), BF16),
            jax.ShapeDtypeStruct((m, GATE_W), BF16),
        ],
        scratch_shapes=[
            pltpu.VMEM((4, CHUNK + T, V7X_LANES), BF16),
            pltpu.VMEM((4, CHUNK + T, V7X_LANES), BF16),
            pltpu.VMEM((RET_HEADS, CHUNK, CHUNK), F32),
            pltpu.VMEM((8 + T, CONV_CH), F32),
        ],
        compiler_params=pltpu.CompilerParams(
            dimension_semantics=("arbitrary", "arbitrary"), vmem_limit_bytes=_vmem_limit(est)),
        name="projmix",
    )(sinks, h, norm_g, w_in, b_gate, rot_tab, dmask, xz, conv_w)


def _merge_kernel(x_ref, br_ref, gate_ref, wb_ref, wo_ref, o_ref):
    acc = None
    for i in range(N_BRANCH):
        y = jnp.dot(br_ref[:, i * BRANCH_W:(i + 1) * BRANCH_W], wb_ref[i], preferred_element_type=F32)
        term = gate_ref[:, i * D_MODEL:(i + 1) * D_MODEL].astype(F32) * y
        acc = term if acc is None else acc + term
    o_ref[...] = x_ref[...] + jnp.dot(acc.astype(BF16), wo_ref[...], preferred_element_type=F32)


def _merge(h, br, gate, w_branch, w_out, layer):
    m = h.shape[0]
    tm = MERGE_TM
    est = (N_BRANCH * BRANCH_W * D_MODEL * 2 + D_MODEL * D_MODEL * 2 + 4 * tm * D_MODEL * 4
           + 2 * tm * 3 * BRANCH_W * 2 + 2 * tm * GATE_W * 2 + 3 * tm * D_MODEL * 4)
    return pl.pallas_call(
        _merge_kernel,
        grid=(m // tm,),
        in_specs=[
            pl.BlockSpec((tm, D_MODEL), lambda i: (i, 0)),
            pl.BlockSpec((tm, 3 * BRANCH_W), lambda i: (i, 0)),
            pl.BlockSpec((tm, GATE_W), lambda i: (i, 0)),
            _resident((None, N_BRANCH, BRANCH_W, D_MODEL), lambda i: (layer, 0, 0, 0)),
            _resident((None, D_MODEL, D_MODEL), lambda i: (layer, 0, 0)),
        ],
        out_specs=pl.BlockSpec((tm, D_MODEL), lambda i: (i, 0)),
        out_shape=jax.ShapeDtypeStruct((m, D_MODEL), F32),
        compiler_params=pltpu.CompilerParams(
            dimension_semantics=("arbitrary",), vmem_limit_bytes=_vmem_limit(est)),
        name="merge",
    )(h, br, gate, w_branch, w_out)


def _ffn_kernel(x_ref, g_ref, wg_ref, wu_ref, wd_ref, gf_ref, o_ref, hid_ref, *, final):
    x = x_ref[...]
    u = _rms(x, g_ref[...]).astype(BF16)
    for c0, cw in FFN_CHUNKS:
        a = jnp.dot(u, wg_ref[:, c0:c0 + cw], preferred_element_type=F32)
        b = jnp.dot(u, wu_ref[:, c0:c0 + cw], preferred_element_type=F32)
        hid_ref[:, c0:c0 + cw] = (a * jax.nn.sigmoid(a) * b).astype(BF16)
    out = x + jnp.dot(hid_ref[...], wd_ref[...], preferred_element_type=F32)
    if final:
        out = _rms(out, gf_ref[...])
    o_ref[...] = out


def _ffn(h, norm_g, w_gate, w_up, w_down, norm_final, layer, final):
    m = h.shape[0]
    tm = FFN_TM
    est = (3 * D_MODEL * D_FF * 2 + 4 * tm * D_MODEL * 4 + tm * D_FF * 2 + 4 * tm * 1024 * 4)
    return pl.pallas_call(
        functools.partial(_ffn_kernel, final=final),
        grid=(m // tm,),
        in_specs=[
            pl.BlockSpec((tm, D_MODEL), lambda i: (i, 0)),
            _resident((None, 1, D_MODEL), lambda i: (layer, 0, 0)),
            _resident((None, D_MODEL, D_FF), lambda i: (layer, 0, 0)),
            _resident((None, D_MODEL, D_FF), lambda i: (layer, 0, 0)),
            _resident((None, D_FF, D_MODEL), lambda i: (layer, 0, 0)),
            _resident((1, D_MODEL), lambda i: (0, 0)),
        ],
        out_specs=pl.BlockSpec((tm, D_MODEL), lambda i: (i, 0)),
        out_shape=jax.ShapeDtypeStruct((m, D_MODEL), F32),
        scratch_shapes=[pltpu.VMEM((tm, D_FF), BF16)],
        compiler_params=pltpu.CompilerParams(
            dimension_semantics=("arbitrary",), vmem_limit_bytes=_vmem_limit(est)),
        name="ffn",
    )(h, norm_g, w_gate, w_up, w_down, norm_final)


def _rotary_table(seq, rot_dim, head_dim, theta):
    half = rot_dim // 2
    inv = theta ** (-jnp.arange(half, dtype=F32) / half)
    ang = jnp.arange(seq, dtype=F32)[:, None] * inv[None, :]
    cos, sin = jnp.cos(ang), jnp.sin(ang)
    zeros = jnp.zeros((seq, half), F32)
    tail0 = jnp.zeros((seq, head_dim - rot_dim), F32)
    tail1 = jnp.ones((seq, head_dim - rot_dim), F32)
    c = jnp.concatenate([cos, cos, tail1], axis=1)
    nxt = jnp.concatenate([-sin, zeros, tail0], axis=1)
    prv = jnp.concatenate([zeros, sin, tail0], axis=1)
    return jnp.concatenate([c, c, nxt, nxt, prv, prv], axis=1)


def _retention_tables():
    log_gamma = jnp.log1p(-jnp.exp2(-(5.0 + jnp.arange(RET_HEADS, dtype=F32))))
    idx = jnp.arange(CHUNK, dtype=F32)
    rel = idx[:, None] - idx[None, :]
    dmask = jnp.exp(jnp.where(rel[None] >= 0, log_gamma[:, None, None] * rel[None], -jnp.inf))
    zeta = jnp.exp(log_gamma[:, None] * (CHUNK - 1.0 - idx)[None])
    xi = jnp.exp(log_gamma[:, None] * (idx + 1.0)[None])
    widen = lambda a: jnp.repeat(a.T, RET_QK_DIM, axis=1)
    return dmask, jnp.concatenate([widen(xi), widen(zeta)], axis=1)


def kernel(x, norm_mix, w_in, attn_sinks, conv_w, w_branch, b_gate, w_out,
           norm_ffn, w_ffn_gate, w_ffn_up, w_ffn_down, norm_final):
    batch, seq, d = x.shape
    assert d == D_MODEL and seq % MIX_T == 0 and (batch * seq) % FFN_TM == 0
    assert w_in.shape == (DEPTH, D_MODEL, D_IN)

    rot_tab = jnp.concatenate([
        _rotary_table(seq, ROPE_DIM, ATTN_HEAD_DIM, ROPE_THETA),
        _rotary_table(seq, RET_QK_DIM, RET_QK_DIM, RET_ROPE_THETA)], axis=1)
    dmask, xz = _retention_tables()

    w_in_b = w_in.astype(BF16)
    w_branch_b = w_branch.astype(BF16)
    w_out_b = w_out.astype(BF16)
    w_gate_b = w_ffn_gate.astype(BF16)
    w_up_b = w_ffn_up.astype(BF16)
    w_down_b = w_ffn_down.astype(BF16)
    norm_mix3 = norm_mix.reshape(DEPTH, 1, D_MODEL)
    norm_ffn3 = norm_ffn.reshape(DEPTH, 1, D_MODEL)
    b_gate3 = b_gate.reshape(DEPTH, 1, GATE_W)
    norm_final2 = norm_final.reshape(1, D_MODEL)

    h = x.reshape(batch * seq, D_MODEL)
    for layer in range(DEPTH):
        br, gate = _projmix(h, attn_sinks, norm_mix3, w_in_b, b_gate3, rot_tab, dmask, xz, conv_w,
                            batch, seq, layer)
        h = _merge(h, br, gate, w_branch_b, w_out_b, layer)
        h = _ffn(h, norm_ffn3, w_gate_b, w_up_b, w_down_b, norm_final2, layer, layer == DEPTH - 1)
    return h.reshape(batch, seq, D_MODEL)
```

```python
import functools

import numpy as np
import jax
import jax.numpy as jnp
from jax import lax
from jax.experimental import pallas as pl
from jax.experimental.pallas import tpu as pltpu

F32 = jnp.float32
BF16 = jnp.bfloat16

D_MODEL = 1024
DEPTH = 4
ATTN_Q_HEADS = 8
ATTN_KV_HEADS = 2
ATTN_HEAD_DIM = 64
WINDOW = 128
ROPE_THETA = 500000.0
ROPE_DIM = ATTN_HEAD_DIM // 4
RET_HEADS = 4
RET_QK_DIM = 64
RET_V_DIM = 128
RET_ROPE_THETA = 10000.0
CONV_CH = 512
N_BRANCH = 3
BRANCH_W = 512
D_FF = 2816
EPS = 1e-6
CHUNK = 128

C_AQ, C_AK, C_AV = 0, 512, 640
C_RQ, C_RK, C_RV, C_RG = 768, 1024, 1280, 1792
C_CB, C_CC, C_CX = 2304, 2816, 3328
MIX_W = 3840
GATE_W = N_BRANCH * D_MODEL
D_IN = MIX_W + GATE_W

V7X_LANES = 128
V7X_BF16_SUBLANES = 16
V7X_VMEM_BYTES = 64 * 1024 * 1024
V7X_SCOPED_VMEM_CAP = 60000 * 1024

MIX_T = 512
GATE_CHUNK = 512
FFN_TM = 512
FFN_CHUNKS = ((0, 1024), (1024, 1024), (2048, 768))

RET_CHUNK_DECAY = tuple(float(np.exp(np.log1p(-(2.0 ** -(5 + h))) * CHUNK)) for h in range(RET_HEADS))


def _vmem_limit(estimate_bytes):
    return int(min(V7X_SCOPED_VMEM_CAP, max(32 * 1024 * 1024, estimate_bytes * 5 // 4)))


def _rms(x, g):
    ms = jnp.mean(x * x, axis=-1, keepdims=True)
    return x * lax.rsqrt(ms + EPS) * g


def _resident(block_shape, index_map):
    return pl.BlockSpec(block_shape, index_map, pipeline_mode=pl.Buffered(1))


def _cast_view(w, n_steps):
    depth, n = w.shape[0], w.shape[-1]
    k = int(np.prod(w.shape[1:-1]))
    rows = k // n_steps if k % (n_steps * V7X_BF16_SUBLANES) == 0 else V7X_LANES
    assert k % rows == 0 and k // rows <= n_steps
    return w.reshape(depth, k // rows, rows, n)


def _cast_specs(views, layer, step_of):
    in_specs, out_specs, out_shapes = [], [], []
    for v in views:
        _, nblk, rows, n = v.shape
        blk = lambda *g, nblk=nblk: jnp.minimum(step_of(*g), nblk - 1)
        in_specs.append(pl.BlockSpec((None, None, rows, n), lambda *g, blk=blk: (layer, blk(*g), 0, 0)))
        out_specs.append(pl.BlockSpec((None, rows, n), lambda *g, blk=blk: (blk(*g), 0, 0)))
        out_shapes.append(jax.ShapeDtypeStruct((nblk, rows, n), BF16))
    return in_specs, out_specs, out_shapes


def _cast_bytes(views):
    return sum(2 * v.shape[2] * v.shape[3] * (4 + 2) for v in views)


def _run_casts(cast_in, cast_out):
    for src, dst in zip(cast_in, cast_out):
        dst[...] = src[...].astype(BF16)


def _rot128(z, cos, coef_next, coef_prev, shift):
    return (z * cos + pltpu.roll(z, V7X_LANES - shift, 1) * coef_next
            + pltpu.roll(z, shift, 1) * coef_prev)


class _Attention:
    def __init__(self, p_attn, rot_ref, sinks_ref, br_ref, kx_ref, vx_ref, t, layer):
        T = MIX_T
        half = ATTN_HEAD_DIM
        self.sinks_ref, self.br_ref, self.kx_ref, self.vx_ref, self.layer = sinks_ref, br_ref, kx_ref, vx_ref, layer
        lo_t = lax.broadcasted_iota(jnp.int32, (T, V7X_LANES), 1) < half
        acos, anext, aprev = rot_ref[:, 0:128], rot_ref[:, 128:256], rot_ref[:, 256:384]
        scale = ATTN_HEAD_DIM ** -0.5
        self.qb = []
        for g in range(4):
            zg = p_attn[:, C_AQ + g * 128:C_AQ + (g + 1) * 128]
            self.qb.append((_rot128(zg, acos, anext, aprev, ROPE_DIM // 2) * scale).astype(BF16))
        k = _rot128(p_attn[:, C_AK:C_AK + 128], acos, anext, aprev, ROPE_DIM // 2)
        v = p_attn[:, C_AV:C_AV + 128]
        for ref, val in ((kx_ref, k), (vx_ref, v)):
            swapped = pltpu.roll(val, half, 1)
            ref[0, CHUNK:CHUNK + T, :] = jnp.where(lo_t, val, 0.0).astype(BF16)
            ref[1, CHUNK:CHUNK + T, :] = jnp.where(lo_t, 0.0, swapped).astype(BF16)
            ref[2, CHUNK:CHUNK + T, :] = jnp.where(lo_t, swapped, 0.0).astype(BF16)
            ref[3, CHUNK:CHUNK + T, :] = jnp.where(lo_t, 0.0, val).astype(BF16)

        row2 = lax.broadcasted_iota(jnp.int32, (2 * CHUNK, 2 * CHUNK), 0) & (CHUNK - 1)
        col2 = lax.broadcasted_iota(jnp.int32, (2 * CHUNK, 2 * CHUNK), 1)
        rel = col2 - row2
        self.band = (rel >= 1) & (rel <= WINDOW)
        first_lim = jnp.where(t == 0, CHUNK, 0)
        self.band_first = self.band & (col2 >= first_lim)
        self.top_rows = lax.broadcasted_iota(jnp.int32, (2 * CHUNK, 1), 0) < CHUNK
        self.lo_2c = lax.broadcasted_iota(jnp.int32, (2 * CHUNK, V7X_LANES), 1) < half

    def scores(self, c):
        r0 = c * CHUNK
        out = []
        for h in range(ATTN_KV_HEADS):
            lhs = jnp.concatenate([self.qb[2 * h][r0:r0 + CHUNK], self.qb[2 * h + 1][r0:r0 + CHUNK]], axis=0)
            for ab in range(2):
                kw = self.kx_ref[2 * h + ab, r0:r0 + 2 * CHUNK, :]
                out.append(lax.dot_general(lhs, kw, (((1,), (1,)), ((), ())), preferred_element_type=F32))
        return out

    def finish(self, c, scores):
        r0 = c * CHUNK
        mask = self.band_first if c == 0 else self.band
        for h in range(ATTN_KV_HEADS):
            outs = []
            for ab in range(2):
                vw = self.vx_ref[2 * h + ab, r0:r0 + 2 * CHUNK, :]
                s = jnp.where(mask, scores[2 * h + ab], -jnp.inf)
                sink = jnp.where(self.top_rows, self.sinks_ref[self.layer, 4 * h + ab],
                                 self.sinks_ref[self.layer, 4 * h + 2 + ab])
                mx = jnp.maximum(jnp.max(s, axis=-1, keepdims=True), sink)
                p = jnp.exp(s - mx)
                den = jnp.sum(p, axis=-1, keepdims=True) + jnp.exp(sink - mx)
                o = jnp.dot(p.astype(BF16), vw, preferred_element_type=F32)
                outs.append(o * (1.0 / den))
            out = jnp.where(self.lo_2c, outs[0], outs[1]).astype(BF16)
            self.br_ref[r0:r0 + CHUNK, (2 * h) * 128:(2 * h + 1) * 128] = out[0:CHUNK]
            self.br_ref[r0:r0 + CHUNK, (2 * h + 1) * 128:(2 * h + 2) * 128] = out[CHUNK:2 * CHUNK]

    def carry(self):
        T = MIX_T
        self.kx_ref[:, 0:CHUNK, :] = self.kx_ref[:, T:T + CHUNK, :]
        self.vx_ref[:, 0:CHUNK, :] = self.vx_ref[:, T:T + CHUNK, :]


class _Retention:
    def __init__(self, p_qk, rot_ref, dmask_ref, xz_ref, br_ref, state_ref):
        half = RET_QK_DIM
        self.dmask_ref, self.xz_ref, self.br_ref, self.state_ref = dmask_ref, xz_ref, br_ref, state_ref
        rcos, rnext, rprev = rot_ref[:, 384:512], rot_ref[:, 512:640], rot_ref[:, 640:768]
        self.lo_c = lax.broadcasted_iota(jnp.int32, (CHUNK, V7X_LANES), 1) < half
        self.top_c = lax.broadcasted_iota(jnp.int32, (CHUNK, V7X_LANES), 0) < half
        kscale = RET_QK_DIM ** -0.5
        self.qp, self.kp = [], []
        for pp in range(RET_HEADS // 2):
            self.qp.append(_rot128(p_qk[:, pp * 128:(pp + 1) * 128], rcos, rnext, rprev, RET_QK_DIM // 2))
            self.kp.append(_rot128(p_qk[:, 256 + pp * 128:256 + (pp + 1) * 128], rcos, rnext, rprev,
                                   RET_QK_DIM // 2) * kscale)

    def scores(self, c):
        r0 = c * CHUNK
        out = []
        for pp in range(RET_HEADS // 2):
            qc = self.qp[pp][r0:r0 + CHUNK]
            kc = self.kp[pp][r0:r0 + CHUNK]
            kbd = jnp.concatenate([jnp.where(self.lo_c, kc, 0.0), jnp.where(self.lo_c, 0.0, kc)],
                                  axis=0).astype(BF16)
            out.append(lax.dot_general(qc.astype(BF16), kbd, (((1,), (1,)), ((), ())),
                                       preferred_element_type=F32))
        return out

    def finish(self, c, scores, p_v, p_g):
        r0 = c * CHUNK
        for pp in range(RET_HEADS // 2):
            qc = self.qp[pp][r0:r0 + CHUNK]
            kc = self.kp[pp][r0:r0 + CHUNK]
            qxi = (qc * self.xz_ref[:, pp * 128:(pp + 1) * 128]).astype(BF16)
            kz = (kc * self.xz_ref[:, 256 + pp * 128:256 + (pp + 1) * 128]).astype(BF16)
            for hh in range(2):
                h = 2 * pp + hh
                vh = p_v[r0:r0 + CHUNK, h * 128:(h + 1) * 128].astype(BF16)
                gh = p_g[r0:r0 + CHUNK, h * 128:(h + 1) * 128]
                att = (scores[pp][:, hh * CHUNK:(hh + 1) * CHUNK] * self.dmask_ref[h]).astype(BF16)
                st = self.state_ref[h]
                y = jnp.dot(jnp.concatenate([att, qxi], axis=1),
                            jnp.concatenate([vh, st.astype(BF16)], axis=0),
                            preferred_element_type=F32)
                kv = lax.dot_general(kz, vh, (((0,), (0,)), ((), ())), preferred_element_type=F32)
                own_rows = self.top_c if hh == 0 else jnp.logical_not(self.top_c)
                self.state_ref[h] = RET_CHUNK_DECAY[h] * st + jnp.where(own_rows, kv, 0.0)
                yn = y * lax.rsqrt(jnp.mean(y * y, axis=-1, keepdims=True) + EPS)
                out = gh * jax.nn.sigmoid(gh) * yn
                self.br_ref[r0:r0 + CHUNK, BRANCH_W + h * 128:BRANCH_W + (h + 1) * 128] = out.astype(BF16)


def _short_conv(cb, cc, cx, convw_ref, br_ref, ubuf_ref):
    T = MIX_T
    u0 = cc * cx
    ubuf_ref[8:8 + T, :] = u0
    u1 = ubuf_ref[7:7 + T, :]
    u2 = ubuf_ref[6:6 + T, :]
    w = convw_ref[...]
    yc = cb * (w[0:1, :] * u2 + w[1:2, :] * u1 + w[2:3, :] * u0)
    br_ref[:, 2 * BRANCH_W:3 * BRANCH_W] = yc.astype(BF16)
    ubuf_ref[0:8, :] = ubuf_ref[T:T + 8, :]


def _projmix_kernel(*refs, layer, n_cast):
    (sinks_ref, x_ref, g_ref, w_ref, bg_ref, rot_ref, dmask_ref, xz_ref, convw_ref), refs = refs[:9], refs[9:]
    cast_in, refs = refs[:n_cast], refs[n_cast:]
    (br_ref, gate_ref), refs = refs[:2], refs[2:]
    cast_out, (kx_ref, vx_ref, state_ref, ubuf_ref) = refs[:n_cast], refs[n_cast:]
    t = pl.program_id(1)

    @pl.when(t == 0)
    def _():
        kx_ref[:, 0:CHUNK, :] = jnp.zeros((4, CHUNK, V7X_LANES), BF16)
        vx_ref[:, 0:CHUNK, :] = jnp.zeros((4, CHUNK, V7X_LANES), BF16)
        state_ref[...] = jnp.zeros_like(state_ref)
        ubuf_ref[0:8, :] = jnp.zeros((8, CONV_CH), F32)

    u = _rms(x_ref[...], g_ref[...]).astype(BF16)
    proj = lambda c0, c1: jnp.dot(u, w_ref[:, c0:c1], preferred_element_type=F32)
    nch = MIX_T // CHUNK

    attn = _Attention(proj(C_AQ, C_RQ), rot_ref, sinks_ref, br_ref, kx_ref, vx_ref, t, layer)
    ret = _Retention(proj(C_RQ, C_RV), rot_ref, dmask_ref, xz_ref, br_ref, state_ref)
    vals = {"rv": proj(C_RV, C_RG)}

    def gate_slice(c0):
        def emit():
            z = proj(MIX_W + c0, MIX_W + c0 + GATE_CHUNK)
            gate_ref[:, c0:c0 + GATE_CHUNK] = jax.nn.sigmoid(z + bg_ref[:, c0:c0 + GATE_CHUNK]).astype(BF16)
        return emit

    def mix_slice(name, c0, c1):
        def emit():
            vals[name] = proj(c0, c1)
            if name == "cx":
                _short_conv(vals["cb"], vals["cc"], vals["cx"], convw_ref, br_ref, ubuf_ref)
        return emit

    slices = [mix_slice("rg", C_RG, C_CB), mix_slice("cb", C_CB, C_CC), mix_slice("cc", C_CC, C_CX),
              mix_slice("cx", C_CX, MIX_W)] + [gate_slice(c0) for c0 in range(0, GATE_W, GATE_CHUNK)]
    units = [("a", 0), ("a", 1), ("r", 0), ("a", 2), ("r", 1), ("a", 3), ("r", 2), ("r", 3)]
    assert len(units) == 2 * nch
    n_slices = len(slices)

    def stage1(kind, c):
        return attn.scores(c) if kind == "a" else ret.scores(c)

    def stage2(kind, c, s):
        if kind == "a":
            attn.finish(c, s)
        else:
            ret.finish(c, s, vals["rv"], vals["rg"])

    pending = stage1(*units[0])
    for i, unit in enumerate(units):
        nxt = stage1(*units[i + 1]) if i + 1 < len(units) else None
        for _ in range(n_slices * (i + 1) // len(units) - n_slices * i // len(units)):
            slices.pop(0)()
        stage2(*unit, pending)
        pending = nxt
    assert not slices
    attn.carry()
    _run_casts(cast_in, cast_out)


def _projmix(h, sinks, norm_g, w_in, b_gate, rot_tab, dmask, xz, conv_w, cast_views, batch, seq, layer):
    T = MIX_T
    nt = seq // T
    m = batch * seq
    cast_in, cast_out, cast_shapes = _cast_specs(cast_views, layer, lambda b, t: b * nt + t)
    est = (D_MODEL * D_IN * 2 + 2 * T * D_MODEL * 4 + 2 * T * 768 * 4 + 2 * 4 * CHUNK * CHUNK * 4
           + 2 * CHUNK * 512 * 4 + 2 * T * 3 * BRANCH_W * 2 + 2 * T * GATE_W * 2
           + 2 * 4 * (CHUNK + T) * 128 * 2 + 4 * CHUNK * CHUNK * 4 + (8 + T) * CONV_CH * 4
           + T * MIX_W * 4 + 8 * T * 512 * 4 + _cast_bytes(cast_views))
    outs = pl.pallas_call(
        functools.partial(_projmix_kernel, layer=layer, n_cast=len(cast_views)),
        grid=(batch, nt),
        in_specs=[
            pl.BlockSpec(memory_space=pltpu.SMEM),
            pl.BlockSpec((T, D_MODEL), lambda b, t: (b * nt + t, 0)),
            _resident((None, 1, D_MODEL), lambda b, t: (layer, 0, 0)),
            _resident((D_MODEL, D_IN), lambda b, t: (0, 0)),
            _resident((None, 1, GATE_W), lambda b, t: (layer, 0, 0)),
            pl.BlockSpec((T, 768), lambda b, t: (t, 0)),
            _resident((RET_HEADS, CHUNK, CHUNK), lambda b, t: (0, 0, 0)),
            _resident((CHUNK, 512), lambda b, t: (0, 0)),
            _resident((None, 3, CONV_CH), lambda b, t: (layer, 0, 0)),
        ] + cast_in,
        out_specs=[
            pl.BlockSpec((T, 3 * BRANCH_W), lambda b, t: (b * nt + t, 0)),
            pl.BlockSpec((T, GATE_W), lambda b, t: (b * nt + t, 0)),
        ] + cast_out,
        out_shape=[
            jax.ShapeDtypeStruct((m, 3 * BRANCH_W), BF16),
            jax.ShapeDtypeStruct((m, GATE_W), BF16),
        ] + cast_shapes,
        scratch_shapes=[
            pltpu.VMEM((4, CHUNK + T, V7X_LANES), BF16),
            pltpu.VMEM((4, CHUNK + T, V7X_LANES), BF16),
            pltpu.VMEM((RET_HEADS, CHUNK, CHUNK), F32),
            pltpu.VMEM((8 + T, CONV_CH), F32),
        ],
        compiler_params=pltpu.CompilerParams(
            dimension_semantics=("arbitrary", "arbitrary"), vmem_limit_bytes=_vmem_limit(est)),
        name="projmix",
    )(sinks, h, norm_g, w_in, b_gate, rot_tab, dmask, xz, conv_w, *cast_views)
    return outs[0], outs[1], [o.reshape(-1, o.shape[-1]) for o in outs[2:]]


def _mergeffn_kernel(*refs, final, n_cast):
    (x_ref, br_ref, gate_ref, wb_ref, wo_ref, g_ref, wg_ref, wu_ref, wd_ref, gf_ref), refs = refs[:10], refs[10:]
    cast_in, refs = refs[:n_cast], refs[n_cast:]
    o_ref, cast_out, hid_ref = refs[0], refs[1:1 + n_cast], refs[1 + n_cast]

    acc = None
    for i in range(N_BRANCH):
        y = jnp.dot(br_ref[:, i * BRANCH_W:(i + 1) * BRANCH_W], wb_ref[i * BRANCH_W:(i + 1) * BRANCH_W, :],
                    preferred_element_type=F32)
        term = gate_ref[:, i * D_MODEL:(i + 1) * D_MODEL].astype(F32) * y
        acc = term if acc is None else acc + term
    x = x_ref[...] + jnp.dot(acc.astype(BF16), wo_ref[...], preferred_element_type=F32)

    u = _rms(x, g_ref[...]).astype(BF16)
    for c0, cw in FFN_CHUNKS:
        a = jnp.dot(u, wg_ref[:, c0:c0 + cw], preferred_element_type=F32)
        b = jnp.dot(u, wu_ref[:, c0:c0 + cw], preferred_element_type=F32)
        hid_ref[:, c0:c0 + cw] = (a * jax.nn.sigmoid(a) * b).astype(BF16)
    out = x + jnp.dot(hid_ref[...], wd_ref[...], preferred_element_type=F32)
    if final:
        out = _rms(out, gf_ref[...])
    o_ref[...] = out
    _run_casts(cast_in, cast_out)


def _mergeffn(h, br, gate, w_branch, w_out, norm_g, w_gate, w_up, w_down, norm_final, cast_views, layer, final):
    m = h.shape[0]
    tm = FFN_TM
    cast_in, cast_out, cast_shapes = _cast_specs(cast_views, layer + 1, lambda i: i)
    est = ((N_BRANCH * BRANCH_W + D_MODEL) * D_MODEL * 2 + 3 * D_MODEL * D_FF * 2 + 4 * tm * D_MODEL * 4
           + 2 * tm * 3 * BRANCH_W * 2 + 2 * tm * GATE_W * 2 + tm * D_FF * 2 + 8 * tm * 1024 * 4
           + _cast_bytes(cast_views))
    outs = pl.pallas_call(
        functools.partial(_mergeffn_kernel, final=final, n_cast=len(cast_views)),
        grid=(m // tm,),
        in_specs=[
            pl.BlockSpec((tm, D_MODEL), lambda i: (i, 0)),
            pl.BlockSpec((tm, 3 * BRANCH_W), lambda i: (i, 0)),
            pl.BlockSpec((tm, GATE_W), lambda i: (i, 0)),
            _resident((N_BRANCH * BRANCH_W, D_MODEL), lambda i: (0, 0)),
            _resident((D_MODEL, D_MODEL), lambda i: (0, 0)),
            _resident((None, 1, D_MODEL), lambda i: (layer, 0, 0)),
            _resident((D_MODEL, D_FF), lambda i: (0, 0)),
            _resident((D_MODEL, D_FF), lambda i: (0, 0)),
            _resident((D_FF, D_MODEL), lambda i: (0, 0)),
            _resident((1, D_MODEL), lambda i: (0, 0)),
        ] + cast_in,
        out_specs=[pl.BlockSpec((tm, D_MODEL), lambda i: (i, 0))] + cast_out,
        out_shape=[jax.ShapeDtypeStruct((m, D_MODEL), F32)] + cast_shapes,
        scratch_shapes=[pltpu.VMEM((tm, D_FF), BF16)],
        compiler_params=pltpu.CompilerParams(
            dimension_semantics=("arbitrary",), vmem_limit_bytes=_vmem_limit(est)),
        name="mergeffn",
    )(h, br, gate, w_branch, w_out, norm_g, w_gate, w_up, w_down, norm_final, *cast_views)
    return outs[0], [o.reshape(-1, o.shape[-1]) for o in outs[1:]]


def _rotary_table(seq, rot_dim, head_dim, theta):
    half = rot_dim // 2
    inv = np.power(np.float32(theta), -np.arange(half, dtype=np.float32) / np.float32(half))
    ang = np.arange(seq, dtype=np.float32)[:, None] * inv[None, :]
    cos, sin = np.cos(ang), np.sin(ang)
    zeros = np.zeros((seq, half), np.float32)
    tail0 = np.zeros((seq, head_dim - rot_dim), np.float32)
    tail1 = np.ones((seq, head_dim - rot_dim), np.float32)
    c = np.concatenate([cos, cos, tail1], axis=1)
    nxt = np.concatenate([-sin, zeros, tail0], axis=1)
    prv = np.concatenate([zeros, sin, tail0], axis=1)
    return np.concatenate([c, c, nxt, nxt, prv, prv], axis=1).astype(np.float32)


def _retention_tables():
    log_gamma = np.log1p(-np.exp2(-(5.0 + np.arange(RET_HEADS, dtype=np.float32)))).astype(np.float32)
    idx = np.arange(CHUNK, dtype=np.float32)
    rel = idx[:, None] - idx[None, :]
    dmask = np.where(rel[None] >= 0, np.exp(log_gamma[:, None, None] * np.maximum(rel[None], 0.0)), 0.0)
    zeta = np.exp(log_gamma[:, None] * (CHUNK - 1.0 - idx)[None])
    xi = np.exp(log_gamma[:, None] * (idx + 1.0)[None])
    widen = lambda a: np.repeat(a.T, RET_QK_DIM, axis=1)
    return dmask.astype(np.float32), np.concatenate([widen(xi), widen(zeta)], axis=1).astype(np.float32)


def kernel(x, norm_mix, w_in, attn_sinks, conv_w, w_branch, b_gate, w_out,
           norm_ffn, w_ffn_gate, w_ffn_up, w_ffn_down, norm_final):
    batch, seq, d = x.shape
    m = batch * seq
    assert d == D_MODEL and seq % MIX_T == 0 and m % FFN_TM == 0 and MIX_T == FFN_TM
    assert w_in.shape == (DEPTH, D_MODEL, D_IN)
    n_steps = m // FFN_TM

    rot_tab = jnp.asarray(np.concatenate([
        _rotary_table(seq, ROPE_DIM, ATTN_HEAD_DIM, ROPE_THETA),
        _rotary_table(seq, RET_QK_DIM, RET_QK_DIM, RET_ROPE_THETA)], axis=1))
    dmask, xz = (jnp.asarray(a) for a in _retention_tables())

    mix_views = [_cast_view(w, n_steps) for w in (w_in, w_branch, w_out)]
    ffn_views = [_cast_view(w, n_steps) for w in (w_ffn_gate, w_ffn_up, w_ffn_down)]
    mix_w = [w_in[0].astype(BF16), w_branch[0].reshape(N_BRANCH * BRANCH_W, D_MODEL).astype(BF16),
             w_out[0].astype(BF16)]
    norm_mix3 = norm_mix.reshape(DEPTH, 1, D_MODEL)
    norm_ffn3 = norm_ffn.reshape(DEPTH, 1, D_MODEL)
    b_gate3 = b_gate.reshape(DEPTH, 1, GATE_W)
    norm_final2 = norm_final.reshape(1, D_MODEL)

    h = x.reshape(m, D_MODEL)
    for layer in range(DEPTH):
        last = layer == DEPTH - 1
        br, gate, ffn_w = _projmix(h, attn_sinks, norm_mix3, mix_w[0], b_gate3, rot_tab, dmask, xz, conv_w,
                                   ffn_views, batch, seq, layer)
        h, mix_w = _mergeffn(h, br, gate, mix_w[1], mix_w[2], norm_ffn3, *ffn_w, norm_final2,
                             [] if last else mix_views, layer, last)
    return h.reshape(batch, seq, D_MODEL)
```

```python
import functools

import numpy as np
import jax
import jax.numpy as jnp
from jax import lax
from jax.experimental import pallas as pl
from jax.experimental.pallas import tpu as pltpu

F32 = jnp.float32
BF16 = jnp.bfloat16

D_MODEL = 1024
DEPTH = 4
ATTN_Q_HEADS = 8
ATTN_KV_HEADS = 2
ATTN_HEAD_DIM = 64
WINDOW = 128
ROPE_THETA = 500000.0
ROPE_DIM = ATTN_HEAD_DIM // 4
RET_HEADS = 4
RET_QK_DIM = 64
RET_V_DIM = 128
RET_ROPE_THETA = 10000.0
CONV_CH = 512
N_BRANCH = 3
BRANCH_W = 512
D_FF = 2816
EPS = 1e-6
CHUNK = 128

C_AQ, C_AK, C_AV = 0, 512, 640
C_RQ, C_RK, C_RV, C_RG = 768, 1024, 1280, 1792
C_CB, C_CC, C_CX = 2304, 2816, 3328
MIX_W = 3840
GATE_W = N_BRANCH * D_MODEL
D_IN = MIX_W + GATE_W

V7X_LANES = 128
V7X_BF16_SUBLANES = 16
V7X_VMEM_BYTES = 64 * 1024 * 1024
V7X_SCOPED_VMEM_CAP = 60000 * 1024

MIX_T = 512
GATE_SLICES = (512,) * 6
MIX_UNITS = (("a", 0), ("a", 1), ("r", 0), ("a", 2), ("r", 1), ("a", 3), ("r", 2), ("r", 3))
MIX_UNIT_SLICES = (1, 1, 0, 1, 0, 1, 1, 1)
MIX_LOOKAHEAD = 1
FFN_TM = 512
FFN_CHUNKS = ((0, 1024), (1024, 1024), (2048, 768))

RET_CHUNK_DECAY = tuple(float(np.exp(np.log1p(-(2.0 ** -(5 + h))) * CHUNK)) for h in range(RET_HEADS))


def _vmem_limit(estimate_bytes):
    return int(min(V7X_SCOPED_VMEM_CAP, max(32 * 1024 * 1024, estimate_bytes * 5 // 4)))


def _rms(x, g):
    ms = jnp.mean(x * x, axis=-1, keepdims=True)
    return x * lax.rsqrt(ms + EPS) * g


def _sigmoid(x):
    return 0.5 * jnp.tanh(0.5 * x) + 0.5


def _resident(block_shape, index_map):
    return pl.BlockSpec(block_shape, index_map, pipeline_mode=pl.Buffered(1))


def _cast_view(w, n_steps):
    depth, n = w.shape[0], w.shape[-1]
    k = int(np.prod(w.shape[1:-1]))
    rows = k // n_steps if k % (n_steps * V7X_BF16_SUBLANES) == 0 else V7X_LANES
    assert k % rows == 0 and k // rows <= n_steps
    return w.reshape(depth, k // rows, rows, n)


def _cast_specs(views, layer, step_of):
    in_specs, out_specs, out_shapes = [], [], []
    for v in views:
        _, nblk, rows, n = v.shape
        blk = lambda *g, nblk=nblk: jnp.minimum(step_of(*g), nblk - 1)
        in_specs.append(pl.BlockSpec((None, None, rows, n), lambda *g, blk=blk: (layer, blk(*g), 0, 0)))
        out_specs.append(pl.BlockSpec((None, rows, n), lambda *g, blk=blk: (blk(*g), 0, 0)))
        out_shapes.append(jax.ShapeDtypeStruct((nblk, rows, n), BF16))
    return in_specs, out_specs, out_shapes


def _cast_bytes(views):
    return sum(2 * v.shape[2] * v.shape[3] * (4 + 2) for v in views)


def _run_casts(cast_in, cast_out):
    for src, dst in zip(cast_in, cast_out):
        dst[...] = src[...].astype(BF16)


def _rot128(z, cos, coef_next, coef_prev, shift):
    return (z * cos + pltpu.roll(z, V7X_LANES - shift, 1) * coef_next
            + pltpu.roll(z, shift, 1) * coef_prev)


class _Attention:
    def __init__(self, p_attn, rot_ref, sinks_ref, br_ref, kx_ref, vx_ref, t, layer):
        T = MIX_T
        half = ATTN_HEAD_DIM
        self.sinks_ref, self.br_ref, self.kx_ref, self.vx_ref, self.layer = sinks_ref, br_ref, kx_ref, vx_ref, layer
        lo_t = lax.broadcasted_iota(jnp.int32, (T, V7X_LANES), 1) < half
        acos, anext, aprev = rot_ref[:, 0:128], rot_ref[:, 128:256], rot_ref[:, 256:384]
        scale = ATTN_HEAD_DIM ** -0.5
        self.qb = []
        for g in range(4):
            zg = p_attn[:, C_AQ + g * 128:C_AQ + (g + 1) * 128]
            self.qb.append((_rot128(zg, acos, anext, aprev, ROPE_DIM // 2) * scale).astype(BF16))
        k = _rot128(p_attn[:, C_AK:C_AK + 128], acos, anext, aprev, ROPE_DIM // 2)
        v = p_attn[:, C_AV:C_AV + 128]
        for ref, val in ((kx_ref, k), (vx_ref, v)):
            swapped = pltpu.roll(val, half, 1)
            ref[0, CHUNK:CHUNK + T, :] = jnp.where(lo_t, val, 0.0).astype(BF16)
            ref[1, CHUNK:CHUNK + T, :] = jnp.where(lo_t, 0.0, swapped).astype(BF16)
            ref[2, CHUNK:CHUNK + T, :] = jnp.where(lo_t, swapped, 0.0).astype(BF16)
            ref[3, CHUNK:CHUNK + T, :] = jnp.where(lo_t, 0.0, val).astype(BF16)

        qi = lax.broadcasted_iota(jnp.int32, (2 * CHUNK, CHUNK), 0) & (CHUNK - 1)
        kj = lax.broadcasted_iota(jnp.int32, (2 * CHUNK, CHUNK), 1)
        self.from_prev = kj > qi
        self.from_prev_bf = jnp.where(self.from_prev, 1.0, 0.0).astype(BF16)
        self.pad_first = self.from_prev & (kj >= jnp.where(t == 0, 0, CHUNK))
        self.top_rows = lax.broadcasted_iota(jnp.int32, (2 * CHUNK, 1), 0) < CHUNK
        self.lo_2c = lax.broadcasted_iota(jnp.int32, (2 * CHUNK, V7X_LANES), 1) < half

    def scores(self, c):
        r0 = c * CHUNK
        out = []
        for h in range(ATTN_KV_HEADS):
            lhs = jnp.concatenate([self.qb[2 * h][r0:r0 + CHUNK], self.qb[2 * h + 1][r0:r0 + CHUNK]], axis=0)
            for ab in range(2):
                kw = self.kx_ref[2 * h + ab, r0:r0 + 2 * CHUNK, :]
                out.append(lax.dot_general(lhs, kw, (((1,), (1,)), ((), ())), preferred_element_type=F32))
        return out

    def finish(self, c, scores):
        r0 = c * CHUNK
        for h in range(ATTN_KV_HEADS):
            outs = []
            for ab in range(2):
                vw = self.vx_ref[2 * h + ab, r0:r0 + 2 * CHUNK, :]
                s = scores[2 * h + ab]
                s = jnp.where(self.from_prev, s[:, 0:CHUNK], s[:, CHUNK:2 * CHUNK])
                if c == 0:
                    s = jnp.where(self.pad_first, -jnp.inf, s)
                sink = jnp.where(self.top_rows, self.sinks_ref[self.layer, 4 * h + ab],
                                 self.sinks_ref[self.layer, 4 * h + 2 + ab])
                mx = jnp.maximum(jnp.max(s, axis=-1, keepdims=True), sink)
                p = jnp.exp(s - mx)
                den = jnp.sum(p, axis=-1, keepdims=True) + jnp.exp(sink - mx)
                pb = p.astype(BF16)
                p_prev = pb * self.from_prev_bf
                o = jnp.dot(jnp.concatenate([p_prev, pb - p_prev], axis=1), vw, preferred_element_type=F32)
                outs.append(o * (1.0 / den))
            out = jnp.where(self.lo_2c, outs[0], outs[1]).astype(BF16)
            self.br_ref[r0:r0 + CHUNK, (2 * h) * 128:(2 * h + 1) * 128] = out[0:CHUNK]
            self.br_ref[r0:r0 + CHUNK, (2 * h + 1) * 128:(2 * h + 2) * 128] = out[CHUNK:2 * CHUNK]

    def carry(self):
        T = MIX_T
        self.kx_ref[:, 0:CHUNK, :] = self.kx_ref[:, T:T + CHUNK, :]
        self.vx_ref[:, 0:CHUNK, :] = self.vx_ref[:, T:T + CHUNK, :]


class _Retention:
    def __init__(self, p_qk, rot_ref, dmask_ref, xz_ref, br_ref, state_ref):
        half = RET_QK_DIM
        self.dmask_ref, self.xz_ref, self.br_ref, self.state_ref = dmask_ref, xz_ref, br_ref, state_ref
        rcos, rnext, rprev = rot_ref[:, 384:512], rot_ref[:, 512:640], rot_ref[:, 640:768]
        self.lo_c = lax.broadcasted_iota(jnp.int32, (CHUNK, V7X_LANES), 1) < half
        self.top_c = lax.broadcasted_iota(jnp.int32, (CHUNK, V7X_LANES), 0) < half
        kscale = RET_QK_DIM ** -0.5
        self.qp, self.kp = [], []
        for pp in range(RET_HEADS // 2):
            self.qp.append(_rot128(p_qk[:, pp * 128:(pp + 1) * 128], rcos, rnext, rprev, RET_QK_DIM // 2))
            self.kp.append(_rot128(p_qk[:, 256 + pp * 128:256 + (pp + 1) * 128], rcos, rnext, rprev,
                                   RET_QK_DIM // 2) * kscale)

    def scores(self, c):
        r0 = c * CHUNK
        out = []
        for pp in range(RET_HEADS // 2):
            qc = self.qp[pp][r0:r0 + CHUNK]
            kc = self.kp[pp][r0:r0 + CHUNK]
            kbd = jnp.concatenate([jnp.where(self.lo_c, kc, 0.0), jnp.where(self.lo_c, 0.0, kc)],
                                  axis=0).astype(BF16)
            out.append(lax.dot_general(qc.astype(BF16), kbd, (((1,), (1,)), ((), ())),
                                       preferred_element_type=F32))
        return out

    def finish(self, c, scores, p_v, p_g):
        r0 = c * CHUNK
        for pp in range(RET_HEADS // 2):
            qc = self.qp[pp][r0:r0 + CHUNK]
            kc = self.kp[pp][r0:r0 + CHUNK]
            qxi = (qc * self.xz_ref[:, pp * 128:(pp + 1) * 128]).astype(BF16)
            kz = (kc * self.xz_ref[:, 256 + pp * 128:256 + (pp + 1) * 128]).astype(BF16)
            for hh in range(2):
                h = 2 * pp + hh
                vh = p_v[r0:r0 + CHUNK, h * 128:(h + 1) * 128].astype(BF16)
                gh = p_g[r0:r0 + CHUNK, h * 128:(h + 1) * 128]
                att = (scores[pp][:, hh * CHUNK:(hh + 1) * CHUNK] * self.dmask_ref[h]).astype(BF16)
                st = self.state_ref[h]
                y = jnp.dot(jnp.concatenate([att, qxi], axis=1),
                            jnp.concatenate([vh, st.astype(BF16)], axis=0),
                            preferred_element_type=F32)
                kv = lax.dot_general(kz, vh, (((0,), (0,)), ((), ())), preferred_element_type=F32)
                own_rows = self.top_c if hh == 0 else jnp.logical_not(self.top_c)
                self.state_ref[h] = RET_CHUNK_DECAY[h] * st + jnp.where(own_rows, kv, 0.0)
                yn = y * lax.rsqrt(jnp.mean(y * y, axis=-1, keepdims=True) + EPS)
                out = gh * _sigmoid(gh) * yn
                self.br_ref[r0:r0 + CHUNK, BRANCH_W + h * 128:BRANCH_W + (h + 1) * 128] = out.astype(BF16)


def _short_conv(cb, cc, cx, convw_ref, br_ref, ubuf_ref):
    T = MIX_T
    u0 = cc * cx
    for k in (1, 2):
        ubuf_ref[k - 1, 8 + k:8 + k + T, :] = u0
    u1 = ubuf_ref[0, 8:8 + T, :]
    u2 = ubuf_ref[1, 8:8 + T, :]
    w = convw_ref[...]
    yc = cb * (w[0:1, :] * u2 + w[1:2, :] * u1 + w[2:3, :] * u0)
    br_ref[:, 2 * BRANCH_W:3 * BRANCH_W] = yc.astype(BF16)
    ubuf_ref[:, 8:16, :] = ubuf_ref[:, T + 8:T + 16, :]


def _projmix_kernel(*refs, layer, n_cast):
    (sinks_ref, x_ref, g_ref, w_ref, bg_ref, rot_ref, dmask_ref, xz_ref, convw_ref), refs = refs[:9], refs[9:]
    cast_in, refs = refs[:n_cast], refs[n_cast:]
    (br_ref, gate_ref), refs = refs[:2], refs[2:]
    cast_out, (kx_ref, vx_ref, state_ref, ubuf_ref) = refs[:n_cast], refs[n_cast:]
    t = pl.program_id(1)

    @pl.when(t == 0)
    def _():
        kx_ref[:, 0:CHUNK, :] = jnp.zeros((4, CHUNK, V7X_LANES), BF16)
        vx_ref[:, 0:CHUNK, :] = jnp.zeros((4, CHUNK, V7X_LANES), BF16)
        state_ref[...] = jnp.zeros_like(state_ref)
        ubuf_ref[:, 8:16, :] = jnp.zeros((2, 8, CONV_CH), F32)
        ubuf_ref[:, MIX_T + 8:MIX_T + 16, :] = jnp.zeros((2, 8, CONV_CH), F32)

    _run_casts(cast_in, cast_out)
    u = _rms(x_ref[...], g_ref[...]).astype(BF16)
    proj = lambda c0, c1: jnp.dot(u, w_ref[:, c0:c1], preferred_element_type=F32)
    nch = MIX_T // CHUNK

    attn = _Attention(proj(C_AQ, C_RQ), rot_ref, sinks_ref, br_ref, kx_ref, vx_ref, t, layer)
    ret = _Retention(proj(C_RQ, C_RV), rot_ref, dmask_ref, xz_ref, br_ref, state_ref)
    _short_conv(proj(C_CB, C_CC), proj(C_CC, C_CX), proj(C_CX, MIX_W), convw_ref, br_ref, ubuf_ref)
    vals = {"rv": proj(C_RV, C_RG)}

    def gate_slice(c0, width):
        def emit():
            z = proj(MIX_W + c0, MIX_W + c0 + width)
            gate_ref[:, c0:c0 + width] = (z + bg_ref[:, c0:c0 + width]).astype(BF16)
        return emit

    def mix_slice(name, c0, c1):
        def emit():
            vals[name] = proj(c0, c1)
        return emit

    slices = [mix_slice("rg", C_RG, C_CB)]
    c0 = 0
    for width in GATE_SLICES:
        slices.append(gate_slice(c0, width))
        c0 += width
    assert c0 == GATE_W
    units = MIX_UNITS
    assert len(units) == 2 * nch == len(MIX_UNIT_SLICES) and len(slices) >= sum(MIX_UNIT_SLICES)

    def stage1(kind, c):
        return attn.scores(c) if kind == "a" else ret.scores(c)

    def stage2(kind, c, s):
        if kind == "a":
            attn.finish(c, s)
        else:
            ret.finish(c, s, vals["rv"], vals["rg"])

    pending = [stage1(*unit) for unit in units[:MIX_LOOKAHEAD]]
    for i, unit in enumerate(units):
        if i + MIX_LOOKAHEAD < len(units):
            pending.append(stage1(*units[i + MIX_LOOKAHEAD]))
        for _ in range(MIX_UNIT_SLICES[i]):
            slices.pop(0)()
        stage2(*unit, pending.pop(0))
    while slices:
        slices.pop(0)()
    attn.carry()


def _projmix(h, sinks, norm_g, w_in, b_gate, rot_tab, dmask, xz, conv_w, cast_views, batch, seq, layer):
    T = MIX_T
    nt = seq // T
    m = batch * seq
    cast_in, cast_out, cast_shapes = _cast_specs(cast_views, layer, lambda b, t: b * nt + t)
    est = (D_MODEL * D_IN * 2 + 2 * T * D_MODEL * 4 + 2 * T * 768 * 4
           + 2 * 4 * CHUNK * CHUNK * 4 + 2 * CHUNK * 512 * 4 + 2 * T * 3 * BRANCH_W * 2 + 2 * T * GATE_W * 2
           + 2 * 4 * (CHUNK + T) * 128 * 2 + 4 * CHUNK * CHUNK * 4 + 2 * (T + 16) * CONV_CH * 4
           + T * MIX_W * 4 + 8 * T * 512 * 4 + _cast_bytes(cast_views))
    outs = pl.pallas_call(
        functools.partial(_projmix_kernel, layer=layer, n_cast=len(cast_views)),
        grid=(batch, nt),
        in_specs=[
            pl.BlockSpec(memory_space=pltpu.SMEM),
            pl.BlockSpec((T, D_MODEL), lambda b, t: (b * nt + t, 0)),
            _resident((None, 1, D_MODEL), lambda b, t: (layer, 0, 0)),
            _resident((D_MODEL, D_IN), lambda b, t: (0, 0)),
            _resident((None, 1, GATE_W), lambda b, t: (layer, 0, 0)),
            pl.BlockSpec((T, 768), lambda b, t: (t, 0)),
            _resident((RET_HEADS, CHUNK, CHUNK), lambda b, t: (0, 0, 0)),
            _resident((CHUNK, 512), lambda b, t: (0, 0)),
            _resident((None, 3, CONV_CH), lambda b, t: (layer, 0, 0)),
        ] + cast_in,
        out_specs=[
            pl.BlockSpec((T, 3 * BRANCH_W), lambda b, t: (b * nt + t, 0)),
            pl.BlockSpec((T, GATE_W), lambda b, t: (b * nt + t, 0)),
        ] + cast_out,
        out_shape=[
            jax.ShapeDtypeStruct((m, 3 * BRANCH_W), BF16),
            jax.ShapeDtypeStruct((m, GATE_W), BF16),
        ] + cast_shapes,
        scratch_shapes=[
            pltpu.VMEM((4, CHUNK + T, V7X_LANES), BF16),
            pltpu.VMEM((4, CHUNK + T, V7X_LANES), BF16),
            pltpu.VMEM((RET_HEADS, CHUNK, CHUNK), F32),
            pltpu.VMEM((2, T + 16, CONV_CH), F32),
        ],
        compiler_params=pltpu.CompilerParams(
            dimension_semantics=("arbitrary", "arbitrary"), vmem_limit_bytes=_vmem_limit(est)),
        name="projmix",
    )(sinks, h, norm_g, w_in, b_gate, rot_tab, dmask, xz, conv_w, *cast_views)
    return outs[0], outs[1], [o.reshape(-1, o.shape[-1]) for o in outs[2:]]


def _mergeffn_kernel(*refs, final, n_cast):
    (x_ref, br_ref, gate_ref, wb_ref, wo_ref, g_ref, wg_ref, wu_ref, wd_ref, gf_ref), refs = refs[:10], refs[10:]
    cast_in, refs = refs[:n_cast], refs[n_cast:]
    o_ref, cast_out, hid_ref = refs[0], refs[1:1 + n_cast], refs[1 + n_cast]

    acc = None
    for i in range(N_BRANCH):
        y = jnp.dot(br_ref[:, i * BRANCH_W:(i + 1) * BRANCH_W], wb_ref[i * BRANCH_W:(i + 1) * BRANCH_W, :],
                    preferred_element_type=F32)
        term = _sigmoid(gate_ref[:, i * D_MODEL:(i + 1) * D_MODEL].astype(F32)) * y
        acc = term if acc is None else acc + term
    x = x_ref[...] + jnp.dot(acc.astype(BF16), wo_ref[...], preferred_element_type=F32)

    u = _rms(x, g_ref[...]).astype(BF16)
    for c0, cw in FFN_CHUNKS:
        a = jnp.dot(u, wg_ref[:, c0:c0 + cw], preferred_element_type=F32)
        b = jnp.dot(u, wu_ref[:, c0:c0 + cw], preferred_element_type=F32)
        hid_ref[:, c0:c0 + cw] = (a * _sigmoid(a) * b).astype(BF16)
    out = x + jnp.dot(hid_ref[...], wd_ref[...], preferred_element_type=F32)
    if final:
        out = _rms(out, gf_ref[...])
    o_ref[...] = out
    _run_casts(cast_in, cast_out)


def _mergeffn(h, br, gate, w_branch, w_out, norm_g, w_gate, w_up, w_down, norm_final, cast_views, layer, final):
    m = h.shape[0]
    tm = FFN_TM
    cast_in, cast_out, cast_shapes = _cast_specs(cast_views, layer + 1, lambda i: i)
    est = ((N_BRANCH * BRANCH_W + D_MODEL) * D_MODEL * 2 + 3 * D_MODEL * D_FF * 2 + 4 * tm * D_MODEL * 4
           + 2 * tm * 3 * BRANCH_W * 2 + 2 * tm * GATE_W * 2 + tm * D_FF * 2 + 8 * tm * 1024 * 4
           + _cast_bytes(cast_views))
    outs = pl.pallas_call(
        functools.partial(_mergeffn_kernel, final=final, n_cast=len(cast_views)),
        grid=(m // tm,),
        in_specs=[
            pl.BlockSpec((tm, D_MODEL), lambda i: (i, 0)),
            pl.BlockSpec((tm, 3 * BRANCH_W), lambda i: (i, 0)),
            pl.BlockSpec((tm, GATE_W), lambda i: (i, 0)),
            _resident((N_BRANCH * BRANCH_W, D_MODEL), lambda i: (0, 0)),
            _resident((D_MODEL, D_MODEL), lambda i: (0, 0)),
            _resident((None, 1, D_MODEL), lambda i: (layer, 0, 0)),
            _resident((D_MODEL, D_FF), lambda i: (0, 0)),
            _resident((D_MODEL, D_FF), lambda i: (0, 0)),
            _resident((D_FF, D_MODEL), lambda i: (0, 0)),
            _resident((1, D_MODEL), lambda i: (0, 0)),
        ] + cast_in,
        out_specs=[pl.BlockSpec((tm, D_MODEL), lambda i: (i, 0))] + cast_out,
        out_shape=[jax.ShapeDtypeStruct((m, D_MODEL), F32)] + cast_shapes,
        scratch_shapes=[pltpu.VMEM((tm, D_FF), BF16)],
        compiler_params=pltpu.CompilerParams(
            dimension_semantics=("arbitrary",), vmem_limit_bytes=_vmem_limit(est)),
        name="mergeffn",
    )(h, br, gate, w_branch, w_out, norm_g, w_gate, w_up, w_down, norm_final, *cast_views)
    return outs[0], [o.reshape(-1, o.shape[-1]) for o in outs[1:]]


def _rotary_table(seq, rot_dim, head_dim, theta):
    half = rot_dim // 2
    inv = np.power(np.float32(theta), -np.arange(half, dtype=np.float32) / np.float32(half))
    ang = np.arange(seq, dtype=np.float32)[:, None] * inv[None, :]
    cos, sin = np.cos(ang), np.sin(ang)
    zeros = np.zeros((seq, half), np.float32)
    tail0 = np.zeros((seq, head_dim - rot_dim), np.float32)
    tail1 = np.ones((seq, head_dim - rot_dim), np.float32)
    c = np.concatenate([cos, cos, tail1], axis=1)
    nxt = np.concatenate([-sin, zeros, tail0], axis=1)
    prv = np.concatenate([zeros, sin, tail0], axis=1)
    return np.concatenate([c, c, nxt, nxt, prv, prv], axis=1).astype(np.float32)


def _retention_tables():
    log_gamma = np.log1p(-np.exp2(-(5.0 + np.arange(RET_HEADS, dtype=np.float32)))).astype(np.float32)
    idx = np.arange(CHUNK, dtype=np.float32)
    rel = idx[:, None] - idx[None, :]
    dmask = np.where(rel[None] >= 0, np.exp(log_gamma[:, None, None] * np.maximum(rel[None], 0.0)), 0.0)
    zeta = np.exp(log_gamma[:, None] * (CHUNK - 1.0 - idx)[None])
    xi = np.exp(log_gamma[:, None] * (idx + 1.0)[None])
    widen = lambda a: np.repeat(a.T, RET_QK_DIM, axis=1)
    return dmask.astype(np.float32), np.concatenate([widen(xi), widen(zeta)], axis=1).astype(np.float32)


def kernel(x, norm_mix, w_in, attn_sinks, conv_w, w_branch, b_gate, w_out,
           norm_ffn, w_ffn_gate, w_ffn_up, w_ffn_down, norm_final):
    batch, seq, d = x.shape
    m = batch * seq
    assert d == D_MODEL and seq % MIX_T == 0 and m % FFN_TM == 0 and MIX_T == FFN_TM
    assert w_in.shape == (DEPTH, D_MODEL, D_IN)
    n_steps = m // FFN_TM

    rot_tab = jnp.asarray(np.concatenate([
        _rotary_table(seq, ROPE_DIM, ATTN_HEAD_DIM, ROPE_THETA),
        _rotary_table(seq, RET_QK_DIM, RET_QK_DIM, RET_ROPE_THETA)], axis=1))
    dmask, xz = (jnp.asarray(a) for a in _retention_tables())

    mix_views = [_cast_view(w, n_steps) for w in (w_in, w_branch, w_out)]
    ffn_views = [_cast_view(w, n_steps) for w in (w_ffn_gate, w_ffn_up, w_ffn_down)]
    mix_w = [w_in[0].astype(BF16), w_branch[0].reshape(N_BRANCH * BRANCH_W, D_MODEL).astype(BF16),
             w_out[0].astype(BF16)]
    norm_mix3 = norm_mix.reshape(DEPTH, 1, D_MODEL)
    norm_ffn3 = norm_ffn.reshape(DEPTH, 1, D_MODEL)
    b_gate3 = b_gate.reshape(DEPTH, 1, GATE_W)
    norm_final2 = norm_final.reshape(1, D_MODEL)

    h = x.reshape(m, D_MODEL)
    for layer in range(DEPTH):
        last = layer == DEPTH - 1
        br, gate, ffn_w = _projmix(h, attn_sinks, norm_mix3, mix_w[0], b_gate3, rot_tab, dmask, xz, conv_w,
                                   ffn_views, batch, seq, layer)
        h, mix_w = _mergeffn(h, br, gate, mix_w[1], mix_w[2], norm_ffn3, *ffn_w, norm_final2,
                             [] if last else mix_views, layer, last)
    return h.reshape(batch, seq, D_MODEL)
```

```python
import functools

import numpy as np
import jax
import jax.numpy as jnp
from jax import lax
from jax.experimental import pallas as pl
from jax.experimental.pallas import tpu as pltpu

F32 = jnp.float32
BF16 = jnp.bfloat16

D_MODEL = 1024
DEPTH = 4
ATTN_Q_HEADS = 8
ATTN_KV_HEADS = 2
ATTN_HEAD_DIM = 64
WINDOW = 128
ROPE_THETA = 500000.0
ROPE_DIM = ATTN_HEAD_DIM // 4
RET_HEADS = 4
RET_QK_DIM = 64
RET_V_DIM = 128
RET_ROPE_THETA = 10000.0
CONV_CH = 512
N_BRANCH = 3
BRANCH_W = 512
D_FF = 2816
EPS = 1e-6
CHUNK = 128

C_AQ, C_AK, C_AV = 0, 512, 640
C_RQ, C_RK, C_RV, C_RG = 768, 1024, 1280, 1792
C_CB, C_CC, C_CX = 2304, 2816, 3328
MIX_W = 3840
GATE_W = N_BRANCH * D_MODEL
D_IN = MIX_W + GATE_W

V7X_LANES = 128
V7X_BF16_SUBLANES = 16
V7X_VMEM_BYTES = 64 * 1024 * 1024
V7X_SCOPED_VMEM_CAP = 60000 * 1024

MIX_T = 512
GATE_SLICES = (512,) * 6
MIX_UNITS = (("a", 0), ("r", 0), ("a", 1), ("r", 1), ("a", 2), ("r", 2), ("a", 3), ("r", 3))
MIX_UNIT_SLICES = (1, 0, 1, 0, 1, 1, 1, 1)
MIX_LOOKAHEAD = 1
FFN_TM = 512
FFN_CHUNKS = ((0, 1024), (1024, 1024), (2048, 768))

RET_CHUNK_DECAY = tuple(float(np.exp(np.log1p(-(2.0 ** -(5 + h))) * CHUNK)) for h in range(RET_HEADS))


def _vmem_limit(estimate_bytes):
    return int(min(V7X_SCOPED_VMEM_CAP, max(32 * 1024 * 1024, estimate_bytes * 5 // 4)))


def _rms(x, g):
    ms = jnp.mean(x * x, axis=-1, keepdims=True)
    return x * lax.rsqrt(ms + EPS) * g


def _sigmoid(x):
    return 0.5 * jnp.tanh(0.5 * x) + 0.5


def _resident(block_shape, index_map):
    return pl.BlockSpec(block_shape, index_map, pipeline_mode=pl.Buffered(1))


def _cast_view(w, n_steps):
    depth, n = w.shape[0], w.shape[-1]
    k = int(np.prod(w.shape[1:-1]))
    rows = k // n_steps if k % (n_steps * V7X_BF16_SUBLANES) == 0 else V7X_LANES
    assert k % rows == 0 and k // rows <= n_steps
    return w.reshape(depth, k // rows, rows, n)


def _cast_specs(views, layer, step_of):
    in_specs, out_specs, out_shapes = [], [], []
    for v in views:
        _, nblk, rows, n = v.shape
        blk = lambda *g, nblk=nblk: jnp.minimum(step_of(*g), nblk - 1)
        in_specs.append(pl.BlockSpec((None, None, rows, n), lambda *g, blk=blk: (layer, blk(*g), 0, 0)))
        out_specs.append(pl.BlockSpec((None, rows, n), lambda *g, blk=blk: (blk(*g), 0, 0)))
        out_shapes.append(jax.ShapeDtypeStruct((nblk, rows, n), BF16))
    return in_specs, out_specs, out_shapes


def _cast_bytes(views):
    return sum(2 * v.shape[2] * v.shape[3] * (4 + 2) for v in views)


def _run_casts(cast_in, cast_out):
    for src, dst in zip(cast_in, cast_out):
        dst[...] = src[...].astype(BF16)


def _rot128(z, cos, coef_next, coef_prev, shift):
    return (z * cos + pltpu.roll(z, V7X_LANES - shift, 1) * coef_next
            + pltpu.roll(z, shift, 1) * coef_prev)


class _Attention:
    def __init__(self, p_attn, rot_ref, sinks_ref, br_ref, kx_ref, vx_ref, t, layer):
        T = MIX_T
        half = ATTN_HEAD_DIM
        self.sinks_ref, self.br_ref, self.kx_ref, self.vx_ref, self.layer = sinks_ref, br_ref, kx_ref, vx_ref, layer
        lo_t = lax.broadcasted_iota(jnp.int32, (T, V7X_LANES), 1) < half
        acos, anext, aprev = rot_ref[:, 0:128], rot_ref[:, 128:256], rot_ref[:, 256:384]
        scale = ATTN_HEAD_DIM ** -0.5
        self.qb = []
        for g in range(4):
            zg = p_attn[:, C_AQ + g * 128:C_AQ + (g + 1) * 128]
            self.qb.append((_rot128(zg, acos, anext, aprev, ROPE_DIM // 2) * scale).astype(BF16))
        k = _rot128(p_attn[:, C_AK:C_AK + 128], acos, anext, aprev, ROPE_DIM // 2)
        v = p_attn[:, C_AV:C_AV + 128]
        for ref, val in ((kx_ref, k), (vx_ref, v)):
            swapped = pltpu.roll(val, half, 1)
            ref[0, CHUNK:CHUNK + T, :] = jnp.where(lo_t, val, 0.0).astype(BF16)
            ref[1, CHUNK:CHUNK + T, :] = jnp.where(lo_t, 0.0, swapped).astype(BF16)
            ref[2, CHUNK:CHUNK + T, :] = jnp.where(lo_t, swapped, 0.0).astype(BF16)
            ref[3, CHUNK:CHUNK + T, :] = jnp.where(lo_t, 0.0, val).astype(BF16)

        qi = lax.broadcasted_iota(jnp.int32, (2 * CHUNK, CHUNK), 0) & (CHUNK - 1)
        kj = lax.broadcasted_iota(jnp.int32, (2 * CHUNK, CHUNK), 1)
        self.from_prev = kj > qi
        self.from_prev_bf = jnp.where(self.from_prev, 1.0, 0.0).astype(BF16)
        self.pad_first = self.from_prev & (kj >= jnp.where(t == 0, 0, CHUNK))
        self.top_rows = lax.broadcasted_iota(jnp.int32, (2 * CHUNK, 1), 0) < CHUNK
        self.lo_2c = lax.broadcasted_iota(jnp.int32, (2 * CHUNK, V7X_LANES), 1) < half

    def scores(self, c):
        r0 = c * CHUNK
        out = []
        for h in range(ATTN_KV_HEADS):
            lhs = jnp.concatenate([self.qb[2 * h][r0:r0 + CHUNK], self.qb[2 * h + 1][r0:r0 + CHUNK]], axis=0)
            for ab in range(2):
                kw = self.kx_ref[2 * h + ab, r0:r0 + 2 * CHUNK, :]
                out.append(lax.dot_general(lhs, kw, (((1,), (1,)), ((), ())), preferred_element_type=F32))
        return out

    def finish(self, c, scores):
        r0 = c * CHUNK
        for h in range(ATTN_KV_HEADS):
            outs = []
            for ab in range(2):
                vw = self.vx_ref[2 * h + ab, r0:r0 + 2 * CHUNK, :]
                s = scores[2 * h + ab]
                s = jnp.where(self.from_prev, s[:, 0:CHUNK], s[:, CHUNK:2 * CHUNK])
                if c == 0:
                    s = jnp.where(self.pad_first, -jnp.inf, s)
                sink = jnp.where(self.top_rows, self.sinks_ref[self.layer, 4 * h + ab],
                                 self.sinks_ref[self.layer, 4 * h + 2 + ab])
                mx = jnp.maximum(jnp.max(s, axis=-1, keepdims=True), sink)
                p = jnp.exp(s - mx)
                den = jnp.sum(p, axis=-1, keepdims=True) + jnp.exp(sink - mx)
                pb = p.astype(BF16)
                p_prev = pb * self.from_prev_bf
                o = jnp.dot(jnp.concatenate([p_prev, pb - p_prev], axis=1), vw, preferred_element_type=F32)
                outs.append(o * (1.0 / den))
            out = jnp.where(self.lo_2c, outs[0], outs[1]).astype(BF16)
            self.br_ref[r0:r0 + CHUNK, (2 * h) * 128:(2 * h + 1) * 128] = out[0:CHUNK]
            self.br_ref[r0:r0 + CHUNK, (2 * h + 1) * 128:(2 * h + 2) * 128] = out[CHUNK:2 * CHUNK]

    def carry(self):
        T = MIX_T
        self.kx_ref[:, 0:CHUNK, :] = self.kx_ref[:, T:T + CHUNK, :]
        self.vx_ref[:, 0:CHUNK, :] = self.vx_ref[:, T:T + CHUNK, :]


class _Retention:
    def __init__(self, p_qk, rot_ref, dmask_ref, xz_ref, br_ref, state_ref):
        half = RET_QK_DIM
        self.dmask_ref, self.xz_ref, self.br_ref, self.state_ref = dmask_ref, xz_ref, br_ref, state_ref
        rcos, rnext, rprev = rot_ref[:, 384:512], rot_ref[:, 512:640], rot_ref[:, 640:768]
        self.lo_c = lax.broadcasted_iota(jnp.int32, (CHUNK, V7X_LANES), 1) < half
        self.top_c = lax.broadcasted_iota(jnp.int32, (CHUNK, V7X_LANES), 0) < half
        kscale = RET_QK_DIM ** -0.5
        self.qp, self.kp = [], []
        for pp in range(RET_HEADS // 2):
            self.qp.append(_rot128(p_qk[:, pp * 128:(pp + 1) * 128], rcos, rnext, rprev, RET_QK_DIM // 2))
            self.kp.append(_rot128(p_qk[:, 256 + pp * 128:256 + (pp + 1) * 128], rcos, rnext, rprev,
                                   RET_QK_DIM // 2) * kscale)

    def scores(self, c):
        r0 = c * CHUNK
        out = []
        for pp in range(RET_HEADS // 2):
            qc = self.qp[pp][r0:r0 + CHUNK]
            kc = self.kp[pp][r0:r0 + CHUNK]
            kbd = jnp.concatenate([jnp.where(self.lo_c, kc, 0.0), jnp.where(self.lo_c, 0.0, kc)],
                                  axis=0).astype(BF16)
            out.append(lax.dot_general(qc.astype(BF16), kbd, (((1,), (1,)), ((), ())),
                                       preferred_element_type=F32))
        return out

    def finish(self, c, scores, p_v, p_g):
        r0 = c * CHUNK
        for pp in range(RET_HEADS // 2):
            qc = self.qp[pp][r0:r0 + CHUNK]
            kc = self.kp[pp][r0:r0 + CHUNK]
            qxi = (qc * self.xz_ref[:, pp * 128:(pp + 1) * 128]).astype(BF16)
            kz = (kc * self.xz_ref[:, 256 + pp * 128:256 + (pp + 1) * 128]).astype(BF16)
            for hh in range(2):
                h = 2 * pp + hh
                vh = p_v[r0:r0 + CHUNK, h * 128:(h + 1) * 128].astype(BF16)
                gh = p_g[r0:r0 + CHUNK, h * 128:(h + 1) * 128]
                att = (scores[pp][:, hh * CHUNK:(hh + 1) * CHUNK] * self.dmask_ref[h]).astype(BF16)
                st = self.state_ref[h]
                y = jnp.dot(jnp.concatenate([att, qxi], axis=1),
                            jnp.concatenate([vh, st.astype(BF16)], axis=0),
                            preferred_element_type=F32)
                kv = lax.dot_general(kz, vh, (((0,), (0,)), ((), ())), preferred_element_type=F32)
                own_rows = self.top_c if hh == 0 else jnp.logical_not(self.top_c)
                self.state_ref[h] = RET_CHUNK_DECAY[h] * st + jnp.where(own_rows, kv, 0.0)
                yn = y * lax.rsqrt(jnp.mean(y * y, axis=-1, keepdims=True) + EPS)
                out = gh * _sigmoid(gh) * yn
                self.br_ref[r0:r0 + CHUNK, BRANCH_W + h * 128:BRANCH_W + (h + 1) * 128] = out.astype(BF16)


def _short_conv(cb, cc, cx, convw_ref, br_ref, ubuf_ref):
    T = MIX_T
    u0 = cc * cx
    for k in (1, 2):
        ubuf_ref[k - 1, 8 + k:8 + k + T, :] = u0
    u1 = ubuf_ref[0, 8:8 + T, :]
    u2 = ubuf_ref[1, 8:8 + T, :]
    w = convw_ref[...]
    yc = cb * (w[0:1, :] * u2 + w[1:2, :] * u1 + w[2:3, :] * u0)
    br_ref[:, 2 * BRANCH_W:3 * BRANCH_W] = yc.astype(BF16)
    ubuf_ref[:, 8:16, :] = ubuf_ref[:, T + 8:T + 16, :]


def _projmix_kernel(*refs, layer, n_cast):
    (sinks_ref, x_ref, g_ref, w_ref, bg_ref, rot_ref, dmask_ref, xz_ref, convw_ref), refs = refs[:9], refs[9:]
    cast_in, refs = refs[:n_cast], refs[n_cast:]
    (br_ref, gate_ref), refs = refs[:2], refs[2:]
    cast_out, (kx_ref, vx_ref, state_ref, ubuf_ref) = refs[:n_cast], refs[n_cast:]
    t = pl.program_id(1)

    @pl.when(t == 0)
    def _():
        kx_ref[:, 0:CHUNK, :] = jnp.zeros((4, CHUNK, V7X_LANES), BF16)
        vx_ref[:, 0:CHUNK, :] = jnp.zeros((4, CHUNK, V7X_LANES), BF16)
        state_ref[...] = jnp.zeros_like(state_ref)
        ubuf_ref[:, 8:16, :] = jnp.zeros((2, 8, CONV_CH), F32)
        ubuf_ref[:, MIX_T + 8:MIX_T + 16, :] = jnp.zeros((2, 8, CONV_CH), F32)

    _run_casts(cast_in, cast_out)
    u = _rms(x_ref[...], g_ref[...]).astype(BF16)
    proj = lambda c0, c1: jnp.dot(u, w_ref[:, c0:c1], preferred_element_type=F32)
    nch = MIX_T // CHUNK

    attn = _Attention(proj(C_AQ, C_RQ), rot_ref, sinks_ref, br_ref, kx_ref, vx_ref, t, layer)
    ret = _Retention(proj(C_RQ, C_RV), rot_ref, dmask_ref, xz_ref, br_ref, state_ref)
    _short_conv(proj(C_CB, C_CC), proj(C_CC, C_CX), proj(C_CX, MIX_W), convw_ref, br_ref, ubuf_ref)
    vals = {"rv": proj(C_RV, C_RG)}

    def gate_slice(c0, width):
        def emit():
            z = proj(MIX_W + c0, MIX_W + c0 + width)
            gate_ref[:, c0:c0 + width] = (z + bg_ref[:, c0:c0 + width]).astype(BF16)
        return emit

    def mix_slice(name, c0, c1):
        def emit():
            vals[name] = proj(c0, c1)
        return emit

    slices = [mix_slice("rg", C_RG, C_CB)]
    c0 = 0
    for width in GATE_SLICES:
        slices.append(gate_slice(c0, width))
        c0 += width
    assert c0 == GATE_W
    units = MIX_UNITS
    assert len(units) == 2 * nch == len(MIX_UNIT_SLICES) and len(slices) >= sum(MIX_UNIT_SLICES)

    def stage1(kind, c):
        return attn.scores(c) if kind == "a" else ret.scores(c)

    def stage2(kind, c, s):
        if kind == "a":
            attn.finish(c, s)
        else:
            ret.finish(c, s, vals["rv"], vals["rg"])

    pending = [stage1(*unit) for unit in units[:MIX_LOOKAHEAD]]
    for i, unit in enumerate(units):
        if i + MIX_LOOKAHEAD < len(units):
            pending.append(stage1(*units[i + MIX_LOOKAHEAD]))
        for _ in range(MIX_UNIT_SLICES[i]):
            slices.pop(0)()
        stage2(*unit, pending.pop(0))
    while slices:
        slices.pop(0)()
    attn.carry()


def _projmix(h, sinks, norm_g, w_in, b_gate, rot_tab, dmask, xz, conv_w, cast_views, batch, seq, layer):
    T = MIX_T
    nt = seq // T
    m = batch * seq
    cast_in, cast_out, cast_shapes = _cast_specs(cast_views, layer, lambda b, t: b * nt + t)
    est = (D_MODEL * D_IN * 2 + 2 * T * D_MODEL * 4 + 2 * T * 768 * 4
           + 2 * 4 * CHUNK * CHUNK * 4 + 2 * CHUNK * 512 * 4 + 2 * T * 3 * BRANCH_W * 2 + 2 * T * GATE_W * 2
           + 2 * 4 * (CHUNK + T) * 128 * 2 + 4 * CHUNK * CHUNK * 4 + 2 * (T + 16) * CONV_CH * 4
           + T * MIX_W * 4 + 8 * T * 512 * 4 + _cast_bytes(cast_views))
    outs = pl.pallas_call(
        functools.partial(_projmix_kernel, layer=layer, n_cast=len(cast_views)),
        grid=(batch, nt),
        in_specs=[
            pl.BlockSpec(memory_space=pltpu.SMEM),
            pl.BlockSpec((T, D_MODEL), lambda b, t: (b * nt + t, 0)),
            _resident((None, 1, D_MODEL), lambda b, t: (layer, 0, 0)),
            _resident((D_MODEL, D_IN), lambda b, t: (0, 0)),
            _resident((None, 1, GATE_W), lambda b, t: (layer, 0, 0)),
            pl.BlockSpec((T, 768), lambda b, t: (t, 0)),
            _resident((RET_HEADS, CHUNK, CHUNK), lambda b, t: (0, 0, 0)),
            _resident((CHUNK, 512), lambda b, t: (0, 0)),
            _resident((None, 3, CONV_CH), lambda b, t: (layer, 0, 0)),
        ] + cast_in,
        out_specs=[
            pl.BlockSpec((T, 3 * BRANCH_W), lambda b, t: (b * nt + t, 0)),
            pl.BlockSpec((T, GATE_W), lambda b, t: (b * nt + t, 0)),
        ] + cast_out,
        out_shape=[
            jax.ShapeDtypeStruct((m, 3 * BRANCH_W), BF16),
            jax.ShapeDtypeStruct((m, GATE_W), BF16),
        ] + cast_shapes,
        scratch_shapes=[
            pltpu.VMEM((4, CHUNK + T, V7X_LANES), BF16),
            pltpu.VMEM((4, CHUNK + T, V7X_LANES), BF16),
            pltpu.VMEM((RET_HEADS, CHUNK, CHUNK), F32),
            pltpu.VMEM((2, T + 16, CONV_CH), F32),
        ],
        compiler_params=pltpu.CompilerParams(
            dimension_semantics=("arbitrary", "arbitrary"), vmem_limit_bytes=_vmem_limit(est)),
        name="projmix",
    )(sinks, h, norm_g, w_in, b_gate, rot_tab, dmask, xz, conv_w, *cast_views)
    return outs[0], outs[1], [o.reshape(-1, o.shape[-1]) for o in outs[2:]]


def _mergeffn_kernel(*refs, final, n_cast):
    (x_ref, br_ref, gate_ref, wb_ref, wo_ref, g_ref, wg_ref, wu_ref, wd_ref, gf_ref), refs = refs[:10], refs[10:]
    cast_in, refs = refs[:n_cast], refs[n_cast:]
    o_ref, cast_out, hid_ref = refs[0], refs[1:1 + n_cast], refs[1 + n_cast]

    acc = None
    for i in range(N_BRANCH):
        y = jnp.dot(br_ref[:, i * BRANCH_W:(i + 1) * BRANCH_W], wb_ref[i * BRANCH_W:(i + 1) * BRANCH_W, :],
                    preferred_element_type=F32)
        term = _sigmoid(gate_ref[:, i * D_MODEL:(i + 1) * D_MODEL].astype(F32)) * y
        acc = term if acc is None else acc + term
    acc = acc.astype(BF16)
    half = FFN_TM // 2
    xs, us = [], []
    for r0 in (0, half):
        xr = x_ref[r0:r0 + half, :] + jnp.dot(acc[r0:r0 + half], wo_ref[...], preferred_element_type=F32)
        xs.append(xr)
        us.append(_rms(xr, g_ref[...]).astype(BF16))
    x = jnp.concatenate(xs, axis=0)
    u = jnp.concatenate(us, axis=0)

    def swiglu(lhs, c0, cw):
        a = jnp.dot(lhs, wg_ref[:, c0:c0 + cw], preferred_element_type=F32)
        b = jnp.dot(lhs, wu_ref[:, c0:c0 + cw], preferred_element_type=F32)
        return (a * _sigmoid(a) * b).astype(BF16)

    c0, cw = FFN_CHUNKS[0]
    for i, r0 in enumerate((0, half)):
        hid_ref[r0:r0 + half, c0:c0 + cw] = swiglu(us[i], c0, cw)
    for c0, cw in FFN_CHUNKS[1:]:
        hid_ref[:, c0:c0 + cw] = swiglu(u, c0, cw)
    out = x + jnp.dot(hid_ref[...], wd_ref[...], preferred_element_type=F32)
    if final:
        out = _rms(out, gf_ref[...])
    o_ref[...] = out
    _run_casts(cast_in, cast_out)


def _mergeffn(h, br, gate, w_branch, w_out, norm_g, w_gate, w_up, w_down, norm_final, cast_views, layer, final):
    m = h.shape[0]
    tm = FFN_TM
    cast_in, cast_out, cast_shapes = _cast_specs(cast_views, layer + 1, lambda i: i)
    est = ((N_BRANCH * BRANCH_W + D_MODEL) * D_MODEL * 2 + 3 * D_MODEL * D_FF * 2 + 4 * tm * D_MODEL * 4
           + 2 * tm * 3 * BRANCH_W * 2 + 2 * tm * GATE_W * 2 + tm * D_FF * 2 + 8 * tm * 1024 * 4
           + _cast_bytes(cast_views))
    outs = pl.pallas_call(
        functools.partial(_mergeffn_kernel, final=final, n_cast=len(cast_views)),
        grid=(m // tm,),
        in_specs=[
            pl.BlockSpec((tm, D_MODEL), lambda i: (i, 0)),
            pl.BlockSpec((tm, 3 * BRANCH_W), lambda i: (i, 0)),
            pl.BlockSpec((tm, GATE_W), lambda i: (i, 0)),
            _resident((N_BRANCH * BRANCH_W, D_MODEL), lambda i: (0, 0)),
            _resident((D_MODEL, D_MODEL), lambda i: (0, 0)),
            _resident((None, 1, D_MODEL), lambda i: (layer, 0, 0)),
            _resident((D_MODEL, D_FF), lambda i: (0, 0)),
            _resident((D_MODEL, D_FF), lambda i: (0, 0)),
            _resident((D_FF, D_MODEL), lambda i: (0, 0)),
            _resident((1, D_MODEL), lambda i: (0, 0)),
        ] + cast_in,
        out_specs=[pl.BlockSpec((tm, D_MODEL), lambda i: (i, 0))] + cast_out,
        out_shape=[jax.ShapeDtypeStruct((m, D_MODEL), F32)] + cast_shapes,
        scratch_shapes=[pltpu.VMEM((tm, D_FF), BF16)],
        compiler_params=pltpu.CompilerParams(
            dimension_semantics=("arbitrary",), vmem_limit_bytes=_vmem_limit(est)),
        name="mergeffn",
    )(h, br, gate, w_branch, w_out, norm_g, w_gate, w_up, w_down, norm_final, *cast_views)
    return outs[0], [o.reshape(-1, o.shape[-1]) for o in outs[1:]]


def _rotary_table(seq, rot_dim, head_dim, theta):
    half = rot_dim // 2
    inv = np.power(np.float32(theta), -np.arange(half, dtype=np.float32) / np.float32(half))
    ang = np.arange(seq, dtype=np.float32)[:, None] * inv[None, :]
    cos, sin = np.cos(ang), np.sin(ang)
    zeros = np.zeros((seq, half), np.float32)
    tail0 = np.zeros((seq, head_dim - rot_dim), np.float32)
    tail1 = np.ones((seq, head_dim - rot_dim), np.float32)
    c = np.concatenate([cos, cos, tail1], axis=1)
    nxt = np.concatenate([-sin, zeros, tail0], axis=1)
    prv = np.concatenate([zeros, sin, tail0], axis=1)
    return np.concatenate([c, c, nxt, nxt, prv, prv], axis=1).astype(np.float32)


def _retention_tables():
    log_gamma = np.log1p(-np.exp2(-(5.0 + np.arange(RET_HEADS, dtype=np.float32)))).astype(np.float32)
    idx = np.arange(CHUNK, dtype=np.float32)
    rel = idx[:, None] - idx[None, :]
    dmask = np.where(rel[None] >= 0, np.exp(log_gamma[:, None, None] * np.maximum(rel[None], 0.0)), 0.0)
    zeta = np.exp(log_gamma[:, None] * (CHUNK - 1.0 - idx)[None])
    xi = np.exp(log_gamma[:, None] * (idx + 1.0)[None])
    widen = lambda a: np.repeat(a.T, RET_QK_DIM, axis=1)
    return dmask.astype(np.float32), np.concatenate([widen(xi), widen(zeta)], axis=1).astype(np.float32)


def kernel(x, norm_mix, w_in, attn_sinks, conv_w, w_branch, b_gate, w_out,
           norm_ffn, w_ffn_gate, w_ffn_up, w_ffn_down, norm_final):
    batch, seq, d = x.shape
    m = batch * seq
    assert d == D_MODEL and seq % MIX_T == 0 and m % FFN_TM == 0 and MIX_T == FFN_TM
    assert w_in.shape == (DEPTH, D_MODEL, D_IN)
    n_steps = m // FFN_TM

    rot_tab = jnp.asarray(np.concatenate([
        _rotary_table(seq, ROPE_DIM, ATTN_HEAD_DIM, ROPE_THETA),
        _rotary_table(seq, RET_QK_DIM, RET_QK_DIM, RET_ROPE_THETA)], axis=1))
    dmask, xz = (jnp.asarray(a) for a in _retention_tables())

    mix_views = [_cast_view(w, n_steps) for w in (w_in, w_branch, w_out)]
    ffn_views = [_cast_view(w, n_steps) for w in (w_ffn_gate, w_ffn_up, w_ffn_down)]
    mix_w = [w_in[0].astype(BF16), w_branch[0].reshape(N_BRANCH * BRANCH_W, D_MODEL).astype(BF16),
             w_out[0].astype(BF16)]
    norm_mix3 = norm_mix.reshape(DEPTH, 1, D_MODEL)
    norm_ffn3 = norm_ffn.reshape(DEPTH, 1, D_MODEL)
    b_gate3 = b_gate.reshape(DEPTH, 1, GATE_W)
    norm_final2 = norm_final.reshape(1, D_MODEL)

    h = x.reshape(m, D_MODEL)
    for layer in range(DEPTH):
        last = layer == DEPTH - 1
        br, gate, ffn_w = _projmix(h, attn_sinks, norm_mix3, mix_w[0], b_gate3, rot_tab, dmask, xz, conv_w,
                                   ffn_views, batch, seq, layer)
        h, mix_w = _mergeffn(h, br, gate, mix_w[1], mix_w[2], norm_ffn3, *ffn_w, norm_final2,
                             [] if last else mix_views, layer, last)
    return h.reshape(batch, seq, D_MODEL)
```

```python
import functools

import numpy as np
import jax
import jax.numpy as jnp
from jax import lax
from jax.experimental import pallas as pl
from jax.experimental.pallas import tpu as pltpu

F32 = jnp.float32
BF16 = jnp.bfloat16

D_MODEL = 1024
DEPTH = 4
ATTN_Q_HEADS = 8
ATTN_KV_HEADS = 2
ATTN_HEAD_DIM = 64
WINDOW = 128
ROPE_THETA = 500000.0
ROPE_DIM = ATTN_HEAD_DIM // 4
RET_HEADS = 4
RET_QK_DIM = 64
RET_V_DIM = 128
RET_ROPE_THETA = 10000.0
CONV_CH = 512
N_BRANCH = 3
BRANCH_W = 512
D_FF = 2816
EPS = 1e-6
CHUNK = 128

C_AQ, C_AK, C_AV = 0, 512, 640
C_RQ, C_RK, C_RV, C_RG = 768, 1024, 1280, 1792
C_CB, C_CC, C_CX = 2304, 2816, 3328
MIX_W = 3840
GATE_W = N_BRANCH * D_MODEL
D_IN = MIX_W + GATE_W

V7X_LANES = 128
V7X_BF16_SUBLANES = 16
V7X_VMEM_BYTES = 64 * 1024 * 1024
V7X_SCOPED_VMEM_CAP = 60000 * 1024

MIX_T = 512
GATE_SLICES = (512,) * 6
MIX_UNITS = (("a", 0), ("r", 0), ("a", 1), ("r", 1), ("a", 2), ("r", 2), ("a", 3), ("r", 3))
MIX_UNIT_SLICES = (1, 0, 1, 0, 1, 1, 1, 1)
MIX_LOOKAHEAD = 1
FFN_TM = 512
FFN_CHUNKS = ((0, 1024), (1024, 1024), (2048, 768))

RET_CHUNK_DECAY = tuple(float(np.exp(np.log1p(-(2.0 ** -(5 + h))) * CHUNK)) for h in range(RET_HEADS))


def _vmem_limit(estimate_bytes):
    return int(min(V7X_SCOPED_VMEM_CAP, max(32 * 1024 * 1024, estimate_bytes * 5 // 4)))


def _rms(x, g):
    ms = jnp.mean(x * x, axis=-1, keepdims=True)
    return x * lax.rsqrt(ms + EPS) * g


def _sigmoid(x):
    return 0.5 * jnp.tanh(0.5 * x) + 0.5


def _resident(block_shape, index_map):
    return pl.BlockSpec(block_shape, index_map, pipeline_mode=pl.Buffered(1))


def _cast_view(w, n_steps):
    depth, n = w.shape[0], w.shape[-1]
    k = int(np.prod(w.shape[1:-1]))
    rows = k // n_steps if k % (n_steps * V7X_BF16_SUBLANES) == 0 else V7X_LANES
    assert k % rows == 0 and k // rows <= n_steps
    return w.reshape(depth, k // rows, rows, n)


def _cast_specs(views, layer, step_of):
    in_specs, out_specs, out_shapes = [], [], []
    for v in views:
        _, nblk, rows, n = v.shape
        blk = lambda *g, nblk=nblk: jnp.minimum(step_of(*g), nblk - 1)
        in_specs.append(pl.BlockSpec((None, None, rows, n), lambda *g, blk=blk: (layer, blk(*g), 0, 0)))
        out_specs.append(pl.BlockSpec((None, rows, n), lambda *g, blk=blk: (blk(*g), 0, 0)))
        out_shapes.append(jax.ShapeDtypeStruct((nblk, rows, n), BF16))
    return in_specs, out_specs, out_shapes


def _cast_bytes(views):
    return sum(2 * v.shape[2] * v.shape[3] * (4 + 2) for v in views)


def _run_casts(cast_in, cast_out):
    for src, dst in zip(cast_in, cast_out):
        dst[...] = src[...].astype(BF16)


def _rot128(z, cos, coef_next, coef_prev, shift):
    return (z * cos + pltpu.roll(z, V7X_LANES - shift, 1) * coef_next
            + pltpu.roll(z, shift, 1) * coef_prev)


class _Attention:
    def __init__(self, p_attn, rot_ref, sinks_ref, br_ref, kx_ref, vx_ref, t, layer):
        T = MIX_T
        half = ATTN_HEAD_DIM
        self.sinks_ref, self.br_ref, self.kx_ref, self.vx_ref, self.layer = sinks_ref, br_ref, kx_ref, vx_ref, layer
        lo_t = lax.broadcasted_iota(jnp.int32, (T, V7X_LANES), 1) < half
        acos, anext, aprev = rot_ref[:, 0:128], rot_ref[:, 128:256], rot_ref[:, 256:384]
        scale = ATTN_HEAD_DIM ** -0.5
        self.qb = []
        for g in range(4):
            zg = p_attn[:, C_AQ + g * 128:C_AQ + (g + 1) * 128]
            self.qb.append((_rot128(zg, acos, anext, aprev, ROPE_DIM // 2) * scale).astype(BF16))
        k = _rot128(p_attn[:, C_AK:C_AK + 128], acos, anext, aprev, ROPE_DIM // 2)
        v = p_attn[:, C_AV:C_AV + 128]
        for ref, val in ((kx_ref, k), (vx_ref, v)):
            swapped = pltpu.roll(val, half, 1)
            ref[0, CHUNK:CHUNK + T, :] = jnp.where(lo_t, val, 0.0).astype(BF16)
            ref[1, CHUNK:CHUNK + T, :] = jnp.where(lo_t, 0.0, swapped).astype(BF16)
            ref[2, CHUNK:CHUNK + T, :] = jnp.where(lo_t, swapped, 0.0).astype(BF16)
            ref[3, CHUNK:CHUNK + T, :] = jnp.where(lo_t, 0.0, val).astype(BF16)

        qi = lax.broadcasted_iota(jnp.int32, (2 * CHUNK, CHUNK), 0) & (CHUNK - 1)
        kj = lax.broadcasted_iota(jnp.int32, (2 * CHUNK, CHUNK), 1)
        self.from_prev = kj > qi
        self.from_prev_bf = jnp.where(self.from_prev, 1.0, 0.0).astype(BF16)
        self.pad_first = self.from_prev & (kj >= jnp.where(t == 0, 0, CHUNK))
        self.top_rows = lax.broadcasted_iota(jnp.int32, (2 * CHUNK, 1), 0) < CHUNK
        self.lo_2c = lax.broadcasted_iota(jnp.int32, (2 * CHUNK, V7X_LANES), 1) < half

    def scores(self, c):
        r0 = c * CHUNK
        out = []
        for h in range(ATTN_KV_HEADS):
            lhs = jnp.concatenate([self.qb[2 * h][r0:r0 + CHUNK], self.qb[2 * h + 1][r0:r0 + CHUNK]], axis=0)
            for ab in range(2):
                kw = self.kx_ref[2 * h + ab, r0:r0 + 2 * CHUNK, :]
                out.append(lax.dot_general(lhs, kw, (((1,), (1,)), ((), ())), preferred_element_type=F32))
        return out

    def finish(self, c, scores):
        r0 = c * CHUNK
        for h in range(ATTN_KV_HEADS):
            outs = []
            for ab in range(2):
                vw = self.vx_ref[2 * h + ab, r0:r0 + 2 * CHUNK, :]
                s = scores[2 * h + ab]
                s = jnp.where(self.from_prev, s[:, 0:CHUNK], s[:, CHUNK:2 * CHUNK])
                if c == 0:
                    s = jnp.where(self.pad_first, -jnp.inf, s)
                sink = jnp.where(self.top_rows, self.sinks_ref[self.layer, 4 * h + ab],
                                 self.sinks_ref[self.layer, 4 * h + 2 + ab])
                mx = jnp.maximum(jnp.max(s, axis=-1, keepdims=True), sink)
                p = jnp.exp(s - mx)
                den = jnp.sum(p, axis=-1, keepdims=True) + jnp.exp(sink - mx)
                pb = p.astype(BF16)
                p_prev = pb * self.from_prev_bf
                o = jnp.dot(jnp.concatenate([p_prev, pb - p_prev], axis=1), vw, preferred_element_type=F32)
                outs.append(o * (1.0 / den))
            out = jnp.where(self.lo_2c, outs[0], outs[1]).astype(BF16)
            self.br_ref[r0:r0 + CHUNK, (2 * h) * 128:(2 * h + 1) * 128] = out[0:CHUNK]
            self.br_ref[r0:r0 + CHUNK, (2 * h + 1) * 128:(2 * h + 2) * 128] = out[CHUNK:2 * CHUNK]

    def carry(self):
        T = MIX_T
        self.kx_ref[:, 0:CHUNK, :] = self.kx_ref[:, T:T + CHUNK, :]
        self.vx_ref[:, 0:CHUNK, :] = self.vx_ref[:, T:T + CHUNK, :]


class _Retention:
    def __init__(self, p_qk, rot_ref, dmask_ref, xz_ref, br_ref, state_ref):
        half = RET_QK_DIM
        self.dmask_ref, self.xz_ref, self.br_ref, self.state_ref = dmask_ref, xz_ref, br_ref, state_ref
        rcos, rnext, rprev = rot_ref[:, 384:512], rot_ref[:, 512:640], rot_ref[:, 640:768]
        self.lo_c = lax.broadcasted_iota(jnp.int32, (CHUNK, V7X_LANES), 1) < half
        self.top_c = lax.broadcasted_iota(jnp.int32, (CHUNK, V7X_LANES), 0) < half
        kscale = RET_QK_DIM ** -0.5
        self.qp, self.kp = [], []
        for pp in range(RET_HEADS // 2):
            self.qp.append(_rot128(p_qk[:, pp * 128:(pp + 1) * 128], rcos, rnext, rprev, RET_QK_DIM // 2))
            self.kp.append(_rot128(p_qk[:, 256 + pp * 128:256 + (pp + 1) * 128], rcos, rnext, rprev,
                                   RET_QK_DIM // 2) * kscale)

    def scores(self, c):
        r0 = c * CHUNK
        out = []
        for pp in range(RET_HEADS // 2):
            qc = self.qp[pp][r0:r0 + CHUNK]
            kc = self.kp[pp][r0:r0 + CHUNK]
            kbd = jnp.concatenate([jnp.where(self.lo_c, kc, 0.0), jnp.where(self.lo_c, 0.0, kc)],
                                  axis=0).astype(BF16)
            out.append(lax.dot_general(qc.astype(BF16), kbd, (((1,), (1,)), ((), ())),
                                       preferred_element_type=F32))
        return out

    def finish(self, c, scores, p_v, p_g):
        r0 = c * CHUNK
        for pp in range(RET_HEADS // 2):
            qc = self.qp[pp][r0:r0 + CHUNK]
            kc = self.kp[pp][r0:r0 + CHUNK]
            qxi = (qc * self.xz_ref[:, pp * 128:(pp + 1) * 128]).astype(BF16)
            kz = (kc * self.xz_ref[:, 256 + pp * 128:256 + (pp + 1) * 128]).astype(BF16)
            for hh in range(2):
                h = 2 * pp + hh
                vh = p_v[r0:r0 + CHUNK, h * 128:(h + 1) * 128].astype(BF16)
                gh = p_g[r0:r0 + CHUNK, h * 128:(h + 1) * 128]
                att = (scores[pp][:, hh * CHUNK:(hh + 1) * CHUNK] * self.dmask_ref[h]).astype(BF16)
                st = self.state_ref[h]
                y = jnp.dot(jnp.concatenate([att, qxi], axis=1),
                            jnp.concatenate([vh, st.astype(BF16)], axis=0),
                            preferred_element_type=F32)
                kv = lax.dot_general(kz, vh, (((0,), (0,)), ((), ())), preferred_element_type=F32)
                own_rows = self.top_c if hh == 0 else jnp.logical_not(self.top_c)
                self.state_ref[h] = RET_CHUNK_DECAY[h] * st + jnp.where(own_rows, kv, 0.0)
                yn = y * lax.rsqrt(jnp.mean(y * y, axis=-1, keepdims=True) + EPS)
                out = gh * _sigmoid(gh) * yn
                self.br_ref[r0:r0 + CHUNK, BRANCH_W + h * 128:BRANCH_W + (h + 1) * 128] = out.astype(BF16)


def _short_conv(cb, cc, cx, convw_ref, br_ref, ubuf_ref):
    T = MIX_T
    u0 = cc * cx
    for k in (1, 2):
        ubuf_ref[k - 1, 8 + k:8 + k + T, :] = u0
    u1 = ubuf_ref[0, 8:8 + T, :]
    u2 = ubuf_ref[1, 8:8 + T, :]
    w = convw_ref[...]
    yc = cb * (w[0:1, :] * u2 + w[1:2, :] * u1 + w[2:3, :] * u0)
    br_ref[:, 2 * BRANCH_W:3 * BRANCH_W] = yc.astype(BF16)
    ubuf_ref[:, 8:16, :] = ubuf_ref[:, T + 8:T + 16, :]


def _projmix_kernel(*refs, layer, n_cast):
    (sinks_ref, x_ref, g_ref, w_ref, bg_ref, rot_ref, dmask_ref, xz_ref, convw_ref), refs = refs[:9], refs[9:]
    cast_in, refs = refs[:n_cast], refs[n_cast:]
    (br_ref, gate_ref), refs = refs[:2], refs[2:]
    cast_out, (kx_ref, vx_ref, state_ref, ubuf_ref) = refs[:n_cast], refs[n_cast:]
    t = pl.program_id(1)

    @pl.when(t == 0)
    def _():
        kx_ref[:, 0:CHUNK, :] = jnp.zeros((4, CHUNK, V7X_LANES), BF16)
        vx_ref[:, 0:CHUNK, :] = jnp.zeros((4, CHUNK, V7X_LANES), BF16)
        state_ref[...] = jnp.zeros_like(state_ref)
        ubuf_ref[:, 8:16, :] = jnp.zeros((2, 8, CONV_CH), F32)
        ubuf_ref[:, MIX_T + 8:MIX_T + 16, :] = jnp.zeros((2, 8, CONV_CH), F32)

    _run_casts(cast_in, cast_out)
    u = _rms(x_ref[...], g_ref[...]).astype(BF16)
    proj = lambda c0, c1: jnp.dot(u, w_ref[:, c0:c1], preferred_element_type=F32)
    nch = MIX_T // CHUNK

    attn = _Attention(proj(C_AQ, C_RQ), rot_ref, sinks_ref, br_ref, kx_ref, vx_ref, t, layer)
    ret = _Retention(proj(C_RQ, C_RV), rot_ref, dmask_ref, xz_ref, br_ref, state_ref)
    _short_conv(proj(C_CB, C_CC), proj(C_CC, C_CX), proj(C_CX, MIX_W), convw_ref, br_ref, ubuf_ref)
    vals = {"rv": proj(C_RV, C_RG)}

    def gate_slice(c0, width):
        def emit():
            z = proj(MIX_W + c0, MIX_W + c0 + width)
            gate_ref[:, c0:c0 + width] = (z + bg_ref[:, c0:c0 + width]).astype(BF16)
        return emit

    def mix_slice(name, c0, c1):
        def emit():
            vals[name] = proj(c0, c1)
        return emit

    slices = [mix_slice("rg", C_RG, C_CB)]
    c0 = 0
    for width in GATE_SLICES:
        slices.append(gate_slice(c0, width))
        c0 += width
    assert c0 == GATE_W
    units = MIX_UNITS
    assert len(units) == 2 * nch == len(MIX_UNIT_SLICES) and len(slices) >= sum(MIX_UNIT_SLICES)

    def stage1(kind, c):
        return attn.scores(c) if kind == "a" else ret.scores(c)

    def stage2(kind, c, s):
        if kind == "a":
            attn.finish(c, s)
        else:
            ret.finish(c, s, vals["rv"], vals["rg"])

    pending = [stage1(*unit) for unit in units[:MIX_LOOKAHEAD]]
    for i, unit in enumerate(units):
        if i + MIX_LOOKAHEAD < len(units):
            pending.append(stage1(*units[i + MIX_LOOKAHEAD]))
        for _ in range(MIX_UNIT_SLICES[i]):
            slices.pop(0)()
        stage2(*unit, pending.pop(0))
    while slices:
        slices.pop(0)()
    attn.carry()


def _projmix(h, sinks, norm_g, w_in, b_gate, rot_tab, dmask, xz, conv_w, cast_views, batch, seq, layer):
    T = MIX_T
    nt = seq // T
    m = batch * seq
    cast_in, cast_out, cast_shapes = _cast_specs(cast_views, layer, lambda b, t: b * nt + t)
    est = (D_MODEL * D_IN * 2 + 2 * T * D_MODEL * 4 + 2 * T * 768 * 4
           + 2 * 4 * CHUNK * CHUNK * 4 + 2 * CHUNK * 512 * 4 + 2 * T * 3 * BRANCH_W * 2 + 2 * T * GATE_W * 2
           + 2 * 4 * (CHUNK + T) * 128 * 2 + 4 * CHUNK * CHUNK * 4 + 2 * (T + 16) * CONV_CH * 4
           + T * MIX_W * 4 + 8 * T * 512 * 4 + _cast_bytes(cast_views))
    outs = pl.pallas_call(
        functools.partial(_projmix_kernel, layer=layer, n_cast=len(cast_views)),
        grid=(batch, nt),
        in_specs=[
            pl.BlockSpec(memory_space=pltpu.SMEM),
            pl.BlockSpec((T, D_MODEL), lambda b, t: (b * nt + t, 0)),
            _resident((None, 1, D_MODEL), lambda b, t: (layer, 0, 0)),
            _resident((D_MODEL, D_IN), lambda b, t: (0, 0)),
            _resident((None, 1, GATE_W), lambda b, t: (layer, 0, 0)),
            pl.BlockSpec((T, 768), lambda b, t: (t, 0)),
            _resident((RET_HEADS, CHUNK, CHUNK), lambda b, t: (0, 0, 0)),
            _resident((CHUNK, 512), lambda b, t: (0, 0)),
            _resident((None, 3, CONV_CH), lambda b, t: (layer, 0, 0)),
        ] + cast_in,
        out_specs=[
            pl.BlockSpec((T, 3 * BRANCH_W), lambda b, t: (b * nt + t, 0)),
            pl.BlockSpec((T, GATE_W), lambda b, t: (b * nt + t, 0)),
        ] + cast_out,
        out_shape=[
            jax.ShapeDtypeStruct((m, 3 * BRANCH_W), BF16),
            jax.ShapeDtypeStruct((m, GATE_W), BF16),
        ] + cast_shapes,
        scratch_shapes=[
            pltpu.VMEM((4, CHUNK + T, V7X_LANES), BF16),
            pltpu.VMEM((4, CHUNK + T, V7X_LANES), BF16),
            pltpu.VMEM((RET_HEADS, CHUNK, CHUNK), F32),
            pltpu.VMEM((2, T + 16, CONV_CH), F32),
        ],
        compiler_params=pltpu.CompilerParams(
            dimension_semantics=("arbitrary", "arbitrary"), vmem_limit_bytes=_vmem_limit(est)),
        name="projmix",
    )(sinks, h, norm_g, w_in, b_gate, rot_tab, dmask, xz, conv_w, *cast_views)
    return outs[0], outs[1], [o.reshape(-1, o.shape[-1]) for o in outs[2:]]


def _mergeffn_kernel(*refs, final, n_cast):
    (x_ref, br_ref, gate_ref, wb_ref, wo_ref, g_ref, wg_ref, wu_ref, wd_ref, gf_ref), refs = refs[:10], refs[10:]
    cast_in, refs = refs[:n_cast], refs[n_cast:]
    o_ref, cast_out, hid_ref = refs[0], refs[1:1 + n_cast], refs[1 + n_cast]

    acc = None
    for i in range(N_BRANCH):
        y = jnp.dot(br_ref[:, i * BRANCH_W:(i + 1) * BRANCH_W], wb_ref[i * BRANCH_W:(i + 1) * BRANCH_W, :],
                    preferred_element_type=F32)
        term = _sigmoid(gate_ref[:, i * D_MODEL:(i + 1) * D_MODEL].astype(F32)) * y
        acc = term if acc is None else acc + term
    acc = acc.astype(BF16)
    half = FFN_TM // 2
    xs, us = [], []
    for r0 in (0, half):
        xr = x_ref[r0:r0 + half, :] + jnp.dot(acc[r0:r0 + half], wo_ref[...], preferred_element_type=F32)
        xs.append(xr)
        us.append(_rms(xr, g_ref[...]).astype(BF16))
    x = jnp.concatenate(xs, axis=0)
    u = jnp.concatenate(us, axis=0)

    def swiglu(lhs, c0, cw):
        a = jnp.dot(lhs, wg_ref[:, c0:c0 + cw], preferred_element_type=F32)
        b = jnp.dot(lhs, wu_ref[:, c0:c0 + cw], preferred_element_type=F32)
        return (a * _sigmoid(a) * b).astype(BF16)

    c0, cw = FFN_CHUNKS[0]
    for i, r0 in enumerate((0, half)):
        hid_ref[r0:r0 + half, c0:c0 + cw] = swiglu(us[i], c0, cw)
    for c0, cw in FFN_CHUNKS[1:]:
        hid_ref[:, c0:c0 + cw] = swiglu(u, c0, cw)
    out = x + jnp.dot(hid_ref[...], wd_ref[...], preferred_element_type=F32)
    if final:
        out = _rms(out, gf_ref[...])
    o_ref[...] = out
    _run_casts(cast_in, cast_out)


def _mergeffn(h, br, gate, w_branch, w_out, norm_g, w_gate, w_up, w_down, norm_final, cast_views, layer, final):
    m = h.shape[0]
    tm = FFN_TM
    cast_in, cast_out, cast_shapes = _cast_specs(cast_views, layer + 1, lambda i: i)
    est = ((N_BRANCH * BRANCH_W + D_MODEL) * D_MODEL * 2 + 3 * D_MODEL * D_FF * 2 + 4 * tm * D_MODEL * 4
           + 2 * tm * 3 * BRANCH_W * 2 + 2 * tm * GATE_W * 2 + tm * D_FF * 2 + 8 * tm * 1024 * 4
           + _cast_bytes(cast_views))
    outs = pl.pallas_call(
        functools.partial(_mergeffn_kernel, final=final, n_cast=len(cast_views)),
        grid=(m // tm,),
        in_specs=[
            pl.BlockSpec((tm, D_MODEL), lambda i: (i, 0)),
            pl.BlockSpec((tm, 3 * BRANCH_W), lambda i: (i, 0)),
            pl.BlockSpec((tm, GATE_W), lambda i: (i, 0)),
            _resident((N_BRANCH * BRANCH_W, D_MODEL), lambda i: (0, 0)),
            _resident((D_MODEL, D_MODEL), lambda i: (0, 0)),
            _resident((None, 1, D_MODEL), lambda i: (layer, 0, 0)),
            _resident((D_MODEL, D_FF), lambda i: (0, 0)),
            _resident((D_MODEL, D_FF), lambda i: (0, 0)),
            _resident((D_FF, D_MODEL), lambda i: (0, 0)),
            _resident((1, D_MODEL), lambda i: (0, 0)),
        ] + cast_in,
        out_specs=[pl.BlockSpec((tm, D_MODEL), lambda i: (i, 0))] + cast_out,
        out_shape=[jax.ShapeDtypeStruct((m, D_MODEL), F32)] + cast_shapes,
        scratch_shapes=[pltpu.VMEM((tm, D_FF), BF16)],
        compiler_params=pltpu.CompilerParams(
            dimension_semantics=("arbitrary",), vmem_limit_bytes=_vmem_limit(est)),
        name="mergeffn",
    )(h, br, gate, w_branch, w_out, norm_g, w_gate, w_up, w_down, norm_final, *cast_views)
    return outs[0], [o.reshape(-1, o.shape[-1]) for o in outs[1:]]


def _rotary_table(seq, rot_dim, head_dim, theta):
    half = rot_dim // 2
    inv = np.power(np.float32(theta), -np.arange(half, dtype=np.float32) / np.float32(half))
    ang = np.arange(seq, dtype=np.float32)[:, None] * inv[None, :]
    cos, sin = np.cos(ang), np.sin(ang)
    zeros = np.zeros((seq, half), np.float32)
    tail0 = np.zeros((seq, head_dim - rot_dim), np.float32)
    tail1 = np.ones((seq, head_dim - rot_dim), np.float32)
    c = np.concatenate([cos, cos, tail1], axis=1)
    nxt = np.concatenate([-sin, zeros, tail0], axis=1)
    prv = np.concatenate([zeros, sin, tail0], axis=1)
    return np.concatenate([c, c, nxt, nxt, prv, prv], axis=1).astype(np.float32)


def _retention_tables():
    log_gamma = np.log1p(-np.exp2(-(5.0 + np.arange(RET_HEADS, dtype=np.float32)))).astype(np.float32)
    idx = np.arange(CHUNK, dtype=np.float32)
    rel = idx[:, None] - idx[None, :]
    dmask = np.where(rel[None] >= 0, np.exp(log_gamma[:, None, None] * np.maximum(rel[None], 0.0)), 0.0)
    zeta = np.exp(log_gamma[:, None] * (CHUNK - 1.0 - idx)[None])
    xi = np.exp(log_gamma[:, None] * (idx + 1.0)[None])
    widen = lambda a: np.repeat(a.T, RET_QK_DIM, axis=1)
    return dmask.astype(np.float32), np.concatenate([widen(xi), widen(zeta)], axis=1).astype(np.float32)


def kernel(x, norm_mix, w_in, attn_sinks, conv_w, w_branch, b_gate, w_out,
           norm_ffn, w_ffn_gate, w_ffn_up, w_ffn_down, norm_final):
    batch, seq, d = x.shape
    m = batch * seq
    assert d == D_MODEL and seq % MIX_T == 0 and m % FFN_TM == 0
    assert w_in.shape == (DEPTH, D_MODEL, D_IN)

    rot_tab = jnp.asarray(np.concatenate([
        _rotary_table(seq, ROPE_DIM, ATTN_HEAD_DIM, ROPE_THETA),
        _rotary_table(seq, RET_QK_DIM, RET_QK_DIM, RET_ROPE_THETA)], axis=1))
    dmask, xz = (jnp.asarray(a) for a in _retention_tables())

    mix_views = [_cast_view(w, m // FFN_TM) for w in (w_in, w_branch, w_out)]
    ffn_views = [_cast_view(w, m // MIX_T) for w in (w_ffn_gate, w_ffn_up, w_ffn_down)]
    mix_w = [w_in[0].astype(BF16), w_branch[0].reshape(N_BRANCH * BRANCH_W, D_MODEL).astype(BF16),
             w_out[0].astype(BF16)]
    norm_mix3 = norm_mix.reshape(DEPTH, 1, D_MODEL)
    norm_ffn3 = norm_ffn.reshape(DEPTH, 1, D_MODEL)
    b_gate3 = b_gate.reshape(DEPTH, 1, GATE_W)
    norm_final2 = norm_final.reshape(1, D_MODEL)

    h = x.reshape(m, D_MODEL)
    for layer in range(DEPTH):
        last = layer == DEPTH - 1
        br, gate, ffn_w = _projmix(h, attn_sinks, norm_mix3, mix_w[0], b_gate3, rot_tab, dmask, xz, conv_w,
                                   ffn_views, batch, seq, layer)
        h, mix_w = _mergeffn(h, br, gate, mix_w[1], mix_w[2], norm_ffn3, *ffn_w, norm_final2,
                             [] if last else mix_views, layer, last)
    return h.reshape(batch, seq, D_MODEL)
```

```python
import functools

import numpy as np
import jax
import jax.numpy as jnp
from jax import lax
from jax.experimental import pallas as pl
from jax.experimental.pallas import tpu as pltpu

F32 = jnp.float32
BF16 = jnp.bfloat16

D_MODEL = 1024
DEPTH = 4
ATTN_Q_HEADS = 8
ATTN_KV_HEADS = 2
ATTN_HEAD_DIM = 64
WINDOW = 128
ROPE_THETA = 500000.0
ROPE_DIM = ATTN_HEAD_DIM // 4
RET_HEADS = 4
RET_QK_DIM = 64
RET_V_DIM = 128
RET_ROPE_THETA = 10000.0
CONV_CH = 512
N_BRANCH = 3
BRANCH_W = 512
D_FF = 2816
EPS = 1e-6
CHUNK = 128

C_AQ, C_AK, C_AV = 0, 512, 640
C_RQ, C_RK, C_RV, C_RG = 768, 1024, 1280, 1792
C_CB, C_CC, C_CX = 2304, 2816, 3328
MIX_W = 3840
GATE_W = N_BRANCH * D_MODEL
D_IN = MIX_W + GATE_W
RET_QK_W = RET_HEADS * RET_QK_DIM

V7X_LANES = 128
V7X_SUBLANES = 8
V7X_BF16_SUBLANES = 16
V7X_VMEM_BYTES = 64 * 1024 * 1024
V7X_SCOPED_VMEM_CAP = 60000 * 1024

ROT_W = 6 * V7X_LANES
XZ_W = 2 * RET_QK_W
CONV_PAD = V7X_SUBLANES

MIX_T = 512
GATE_SLICES = (512,) * 6
MIX_UNITS = (("a", 0), ("r", 0), ("a", 1), ("r", 1), ("a", 2), ("r", 2), ("a", 3), ("r", 3))
MIX_UNIT_SLICES = (1, 0, 1, 0, 1, 1, 1, 1)
MIX_LOOKAHEAD = 1
FFN_TM = 512
FFN_CHUNKS = ((0, 1024), (1024, 1024), (2048, 768))

RET_CHUNK_DECAY = tuple(float(np.exp(np.log1p(-(2.0 ** -(5 + h))) * CHUNK)) for h in range(RET_HEADS))


def _vmem_limit(estimate_bytes):
    return int(min(V7X_SCOPED_VMEM_CAP, max(32 * 1024 * 1024, estimate_bytes * 5 // 4)))


def _rms(x, g):
    ms = jnp.mean(x * x, axis=-1, keepdims=True)
    return x * lax.rsqrt(ms + EPS) * g


def _sigmoid(x):
    return 0.5 * jnp.tanh(0.5 * x) + 0.5


def _resident(block_shape, index_map):
    return pl.BlockSpec(block_shape, index_map, pipeline_mode=pl.Buffered(1))


def _cast_view(w, n_steps):
    depth, n = w.shape[0], w.shape[-1]
    k = int(np.prod(w.shape[1:-1]))
    rows = k // n_steps if k % (n_steps * V7X_BF16_SUBLANES) == 0 else V7X_LANES
    assert k % rows == 0 and k // rows <= n_steps
    return w.reshape(depth, k // rows, rows, n)


def _cast_specs(views, layer, step_of):
    in_specs, out_specs, out_shapes = [], [], []
    for v in views:
        _, nblk, rows, n = v.shape
        blk = lambda *g, nblk=nblk: jnp.minimum(step_of(*g), nblk - 1)
        in_specs.append(pl.BlockSpec((None, None, rows, n), lambda *g, blk=blk: (layer, blk(*g), 0, 0)))
        out_specs.append(pl.BlockSpec((None, rows, n), lambda *g, blk=blk: (blk(*g), 0, 0)))
        out_shapes.append(jax.ShapeDtypeStruct((nblk, rows, n), BF16))
    return in_specs, out_specs, out_shapes


def _cast_bytes(views):
    return sum(2 * v.shape[2] * v.shape[3] * (4 + 2) for v in views)


def _run_casts(cast_in, cast_out):
    for src, dst in zip(cast_in, cast_out):
        dst[...] = src[...].astype(BF16)


def _rot128(z, cos, coef_next, coef_prev, shift):
    return (z * cos + pltpu.roll(z, V7X_LANES - shift, 1) * coef_next
            + pltpu.roll(z, shift, 1) * coef_prev)


class _Attention:
    def __init__(self, p_attn, rot_ref, sinks_ref, br_ref, kx_ref, vx_ref, t, layer):
        T = MIX_T
        half = ATTN_HEAD_DIM
        self.sinks_ref, self.br_ref, self.kx_ref, self.vx_ref, self.layer = sinks_ref, br_ref, kx_ref, vx_ref, layer
        lo_t = lax.broadcasted_iota(jnp.int32, (T, V7X_LANES), 1) < half
        acos, anext, aprev = (rot_ref[:, i * V7X_LANES:(i + 1) * V7X_LANES] for i in range(3))
        scale = ATTN_HEAD_DIM ** -0.5
        self.qb = []
        for g in range(4):
            zg = p_attn[:, C_AQ + g * V7X_LANES:C_AQ + (g + 1) * V7X_LANES]
            self.qb.append((_rot128(zg, acos, anext, aprev, ROPE_DIM // 2) * scale).astype(BF16))
        k = _rot128(p_attn[:, C_AK:C_AK + V7X_LANES], acos, anext, aprev, ROPE_DIM // 2)
        v = p_attn[:, C_AV:C_AV + V7X_LANES]
        for ref, val in ((kx_ref, k), (vx_ref, v)):
            swapped = pltpu.roll(val, half, 1)
            ref[0, CHUNK:CHUNK + T, :] = jnp.where(lo_t, val, 0.0).astype(BF16)
            ref[1, CHUNK:CHUNK + T, :] = jnp.where(lo_t, 0.0, swapped).astype(BF16)
            ref[2, CHUNK:CHUNK + T, :] = jnp.where(lo_t, swapped, 0.0).astype(BF16)
            ref[3, CHUNK:CHUNK + T, :] = jnp.where(lo_t, 0.0, val).astype(BF16)

        qi = lax.broadcasted_iota(jnp.int32, (2 * CHUNK, CHUNK), 0) & (CHUNK - 1)
        kj = lax.broadcasted_iota(jnp.int32, (2 * CHUNK, CHUNK), 1)
        self.from_prev = kj > qi
        self.from_prev_bf = jnp.where(self.from_prev, 1.0, 0.0).astype(BF16)
        self.pad_first = self.from_prev & (kj >= jnp.where(t == 0, 0, CHUNK))
        self.top_rows = lax.broadcasted_iota(jnp.int32, (2 * CHUNK, 1), 0) < CHUNK
        self.lo_2c = lax.broadcasted_iota(jnp.int32, (2 * CHUNK, V7X_LANES), 1) < half

    def scores(self, c):
        r0 = c * CHUNK
        out = []
        for h in range(ATTN_KV_HEADS):
            lhs = jnp.concatenate([self.qb[2 * h][r0:r0 + CHUNK], self.qb[2 * h + 1][r0:r0 + CHUNK]], axis=0)
            for ab in range(2):
                kw = self.kx_ref[2 * h + ab, r0:r0 + 2 * CHUNK, :]
                out.append(lax.dot_general(lhs, kw, (((1,), (1,)), ((), ())), preferred_element_type=F32))
        return out

    def finish(self, c, scores):
        r0 = c * CHUNK
        for h in range(ATTN_KV_HEADS):
            outs = []
            for ab in range(2):
                vw = self.vx_ref[2 * h + ab, r0:r0 + 2 * CHUNK, :]
                s = scores[2 * h + ab]
                s = jnp.where(self.from_prev, s[:, 0:CHUNK], s[:, CHUNK:2 * CHUNK])
                if c == 0:
                    s = jnp.where(self.pad_first, -jnp.inf, s)
                sink = jnp.where(self.top_rows, self.sinks_ref[self.layer, 4 * h + ab],
                                 self.sinks_ref[self.layer, 4 * h + 2 + ab])
                mx = jnp.maximum(jnp.max(s, axis=-1, keepdims=True), sink)
                p = jnp.exp(s - mx)
                den = jnp.sum(p, axis=-1, keepdims=True) + jnp.exp(sink - mx)
                pb = p.astype(BF16)
                p_prev = pb * self.from_prev_bf
                o = jnp.dot(jnp.concatenate([p_prev, pb - p_prev], axis=1), vw, preferred_element_type=F32)
                outs.append(o * (1.0 / den))
            out = jnp.where(self.lo_2c, outs[0], outs[1]).astype(BF16)
            self.br_ref[r0:r0 + CHUNK, (2 * h) * V7X_LANES:(2 * h + 1) * V7X_LANES] = out[0:CHUNK]
            self.br_ref[r0:r0 + CHUNK, (2 * h + 1) * V7X_LANES:(2 * h + 2) * V7X_LANES] = out[CHUNK:2 * CHUNK]

    def carry(self):
        T = MIX_T
        self.kx_ref[:, 0:CHUNK, :] = self.kx_ref[:, T:T + CHUNK, :]
        self.vx_ref[:, 0:CHUNK, :] = self.vx_ref[:, T:T + CHUNK, :]


class _Retention:
    def __init__(self, p_qk, rot_ref, dmask_ref, xz_ref, br_ref, state_ref):
        half = RET_QK_DIM
        self.dmask_ref, self.xz_ref, self.br_ref, self.state_ref = dmask_ref, xz_ref, br_ref, state_ref
        rcos, rnext, rprev = (rot_ref[:, i * V7X_LANES:(i + 1) * V7X_LANES] for i in range(3, 6))
        self.lo_c = lax.broadcasted_iota(jnp.int32, (CHUNK, V7X_LANES), 1) < half
        self.top_c = lax.broadcasted_iota(jnp.int32, (CHUNK, V7X_LANES), 0) < half
        kscale = RET_QK_DIM ** -0.5
        self.qp, self.kp = [], []
        for pp in range(RET_HEADS // 2):
            q0, k0 = pp * V7X_LANES, RET_QK_W + pp * V7X_LANES
            self.qp.append(_rot128(p_qk[:, q0:q0 + V7X_LANES], rcos, rnext, rprev, RET_QK_DIM // 2))
            self.kp.append(_rot128(p_qk[:, k0:k0 + V7X_LANES], rcos, rnext, rprev, RET_QK_DIM // 2) * kscale)

    def scores(self, c):
        r0 = c * CHUNK
        out = []
        for pp in range(RET_HEADS // 2):
            qc = self.qp[pp][r0:r0 + CHUNK]
            kc = self.kp[pp][r0:r0 + CHUNK]
            kbd = jnp.concatenate([jnp.where(self.lo_c, kc, 0.0), jnp.where(self.lo_c, 0.0, kc)],
                                  axis=0).astype(BF16)
            out.append(lax.dot_general(qc.astype(BF16), kbd, (((1,), (1,)), ((), ())),
                                       preferred_element_type=F32))
        return out

    def finish(self, c, scores, p_v, p_g):
        r0 = c * CHUNK
        for pp in range(RET_HEADS // 2):
            qc = self.qp[pp][r0:r0 + CHUNK]
            kc = self.kp[pp][r0:r0 + CHUNK]
            qxi = (qc * self.xz_ref[:, pp * V7X_LANES:(pp + 1) * V7X_LANES]).astype(BF16)
            kz = (kc * self.xz_ref[:, RET_QK_W + pp * V7X_LANES:RET_QK_W + (pp + 1) * V7X_LANES]).astype(BF16)
            for hh in range(2):
                h = 2 * pp + hh
                vh = p_v[r0:r0 + CHUNK, h * V7X_LANES:(h + 1) * V7X_LANES].astype(BF16)
                gh = p_g[r0:r0 + CHUNK, h * V7X_LANES:(h + 1) * V7X_LANES]
                att = (scores[pp][:, hh * CHUNK:(hh + 1) * CHUNK] * self.dmask_ref[h]).astype(BF16)
                st = self.state_ref[h]
                y = jnp.dot(jnp.concatenate([att, qxi], axis=1),
                            jnp.concatenate([vh, st.astype(BF16)], axis=0),
                            preferred_element_type=F32)
                kv = lax.dot_general(kz, vh, (((0,), (0,)), ((), ())), preferred_element_type=F32)
                own_rows = self.top_c if hh == 0 else jnp.logical_not(self.top_c)
                self.state_ref[h] = RET_CHUNK_DECAY[h] * st + jnp.where(own_rows, kv, 0.0)
                yn = y * lax.rsqrt(jnp.mean(y * y, axis=-1, keepdims=True) + EPS)
                out = gh * _sigmoid(gh) * yn
                self.br_ref[r0:r0 + CHUNK, BRANCH_W + h * V7X_LANES:BRANCH_W + (h + 1) * V7X_LANES] = out.astype(BF16)


def _short_conv(cb, cc, cx, convw_ref, br_ref, ubuf_ref):
    T, P = MIX_T, CONV_PAD
    u0 = cc * cx
    for k in (1, 2):
        ubuf_ref[k - 1, P + k:P + k + T, :] = u0
    u1 = ubuf_ref[0, P:P + T, :]
    u2 = ubuf_ref[1, P:P + T, :]
    w = convw_ref[...]
    yc = cb * (w[0:1, :] * u2 + w[1:2, :] * u1 + w[2:3, :] * u0)
    br_ref[:, 2 * BRANCH_W:3 * BRANCH_W] = yc.astype(BF16)
    ubuf_ref[:, P:2 * P, :] = ubuf_ref[:, T + P:T + 2 * P, :]


def _projmix_kernel(*refs, layer, n_cast):
    (sinks_ref, x_ref, g_ref, w_ref, bg_ref, rot_ref, dmask_ref, xz_ref, convw_ref), refs = refs[:9], refs[9:]
    cast_in, refs = refs[:n_cast], refs[n_cast:]
    (br_ref, gate_ref), refs = refs[:2], refs[2:]
    cast_out, (kx_ref, vx_ref, state_ref, ubuf_ref) = refs[:n_cast], refs[n_cast:]
    t = pl.program_id(1)

    @pl.when(t == 0)
    def _():
        kx_ref[:, 0:CHUNK, :] = jnp.zeros((4, CHUNK, V7X_LANES), BF16)
        vx_ref[:, 0:CHUNK, :] = jnp.zeros((4, CHUNK, V7X_LANES), BF16)
        state_ref[...] = jnp.zeros_like(state_ref)
        ubuf_ref[:, CONV_PAD:2 * CONV_PAD, :] = jnp.zeros((2, CONV_PAD, CONV_CH), F32)
        ubuf_ref[:, MIX_T + CONV_PAD:MIX_T + 2 * CONV_PAD, :] = jnp.zeros((2, CONV_PAD, CONV_CH), F32)

    _run_casts(cast_in, cast_out)
    u = _rms(x_ref[...], g_ref[...]).astype(BF16)
    proj = lambda c0, c1: jnp.dot(u, w_ref[:, c0:c1], preferred_element_type=F32)
    nch = MIX_T // CHUNK

    attn = _Attention(proj(C_AQ, C_RQ), rot_ref, sinks_ref, br_ref, kx_ref, vx_ref, t, layer)
    ret = _Retention(proj(C_RQ, C_RV), rot_ref, dmask_ref, xz_ref, br_ref, state_ref)
    _short_conv(proj(C_CB, C_CC), proj(C_CC, C_CX), proj(C_CX, MIX_W), convw_ref, br_ref, ubuf_ref)
    vals = {"rv": proj(C_RV, C_RG)}

    def gate_slice(c0, width):
        def emit():
            z = proj(MIX_W + c0, MIX_W + c0 + width)
            gate_ref[:, c0:c0 + width] = (z + bg_ref[:, c0:c0 + width]).astype(BF16)
        return emit

    def mix_slice(name, c0, c1):
        def emit():
            vals[name] = proj(c0, c1)
        return emit

    slices = [mix_slice("rg", C_RG, C_CB)]
    c0 = 0
    for width in GATE_SLICES:
        slices.append(gate_slice(c0, width))
        c0 += width
    assert c0 == GATE_W
    units = MIX_UNITS
    assert len(units) == 2 * nch == len(MIX_UNIT_SLICES) and len(slices) >= sum(MIX_UNIT_SLICES)

    def stage1(kind, c):
        return attn.scores(c) if kind == "a" else ret.scores(c)

    def stage2(kind, c, s):
        if kind == "a":
            attn.finish(c, s)
        else:
            ret.finish(c, s, vals["rv"], vals["rg"])

    pending = [stage1(*unit) for unit in units[:MIX_LOOKAHEAD]]
    for i, unit in enumerate(units):
        if i + MIX_LOOKAHEAD < len(units):
            pending.append(stage1(*units[i + MIX_LOOKAHEAD]))
        for _ in range(MIX_UNIT_SLICES[i]):
            slices.pop(0)()
        stage2(*unit, pending.pop(0))
    while slices:
        slices.pop(0)()
    attn.carry()


def _projmix(h, sinks, norm_g, w_in, b_gate, rot_tab, dmask, xz, conv_w, cast_views, batch, seq, layer):
    T = MIX_T
    nt = seq // T
    m = batch * seq
    cast_in, cast_out, cast_shapes = _cast_specs(cast_views, layer, lambda b, t: b * nt + t)
    est = (D_MODEL * D_IN * 2 + 2 * T * D_MODEL * 4 + 2 * T * ROT_W * 4
           + RET_HEADS * CHUNK * CHUNK * 4 + CHUNK * XZ_W * 4 + 2 * T * 3 * BRANCH_W * 2 + 2 * T * GATE_W * 2
           + 2 * 4 * (CHUNK + T) * V7X_LANES * 2 + RET_HEADS * CHUNK * CHUNK * 4 + 2 * (T + 2 * CONV_PAD) * CONV_CH * 4
           + 2 * T * MIX_W * 4 + _cast_bytes(cast_views))
    outs = pl.pallas_call(
        functools.partial(_projmix_kernel, layer=layer, n_cast=len(cast_views)),
        grid=(batch, nt),
        in_specs=[
            pl.BlockSpec(memory_space=pltpu.SMEM),
            pl.BlockSpec((T, D_MODEL), lambda b, t: (b * nt + t, 0)),
            _resident((None, 1, D_MODEL), lambda b, t: (layer, 0, 0)),
            _resident((D_MODEL, D_IN), lambda b, t: (0, 0)),
            _resident((None, 1, GATE_W), lambda b, t: (layer, 0, 0)),
            pl.BlockSpec((T, ROT_W), lambda b, t: (t, 0)),
            _resident((RET_HEADS, CHUNK, CHUNK), lambda b, t: (0, 0, 0)),
            _resident((CHUNK, XZ_W), lambda b, t: (0, 0)),
            _resident((None, 3, CONV_CH), lambda b, t: (layer, 0, 0)),
        ] + cast_in,
        out_specs=[
            pl.BlockSpec((T, 3 * BRANCH_W), lambda b, t: (b * nt + t, 0)),
            pl.BlockSpec((T, GATE_W), lambda b, t: (b * nt + t, 0)),
        ] + cast_out,
        out_shape=[
            jax.ShapeDtypeStruct((m, 3 * BRANCH_W), BF16),
            jax.ShapeDtypeStruct((m, GATE_W), BF16),
        ] + cast_shapes,
        scratch_shapes=[
            pltpu.VMEM((4, CHUNK + T, V7X_LANES), BF16),
            pltpu.VMEM((4, CHUNK + T, V7X_LANES), BF16),
            pltpu.VMEM((RET_HEADS, CHUNK, CHUNK), F32),
            pltpu.VMEM((2, T + 2 * CONV_PAD, CONV_CH), F32),
        ],
        compiler_params=pltpu.CompilerParams(
            dimension_semantics=("arbitrary", "arbitrary"), vmem_limit_bytes=_vmem_limit(est)),
        name="projmix",
    )(sinks, h, norm_g, w_in, b_gate, rot_tab, dmask, xz, conv_w, *cast_views)
    return outs[0], outs[1], [o.reshape(-1, o.shape[-1]) for o in outs[2:]]


def _mergeffn_kernel(*refs, final, n_cast):
    (x_ref, br_ref, gate_ref, wb_ref, wo_ref, g_ref, wg_ref, wu_ref, wd_ref, gf_ref), refs = refs[:10], refs[10:]
    cast_in, refs = refs[:n_cast], refs[n_cast:]
    o_ref, cast_out, hid_ref = refs[0], refs[1:1 + n_cast], refs[1 + n_cast]

    acc = None
    for i in range(N_BRANCH):
        y = jnp.dot(br_ref[:, i * BRANCH_W:(i + 1) * BRANCH_W], wb_ref[i * BRANCH_W:(i + 1) * BRANCH_W, :],
                    preferred_element_type=F32)
        term = _sigmoid(gate_ref[:, i * D_MODEL:(i + 1) * D_MODEL].astype(F32)) * y
        acc = term if acc is None else acc + term
    acc = acc.astype(BF16)
    half = FFN_TM // 2
    xs, us = [], []
    for r0 in (0, half):
        xr = x_ref[r0:r0 + half, :] + jnp.dot(acc[r0:r0 + half], wo_ref[...], preferred_element_type=F32)
        xs.append(xr)
        us.append(_rms(xr, g_ref[...]).astype(BF16))
    x = jnp.concatenate(xs, axis=0)
    u = jnp.concatenate(us, axis=0)

    def swiglu(lhs, c0, cw):
        a = jnp.dot(lhs, wg_ref[:, c0:c0 + cw], preferred_element_type=F32)
        b = jnp.dot(lhs, wu_ref[:, c0:c0 + cw], preferred_element_type=F32)
        return (a * _sigmoid(a) * b).astype(BF16)

    c0, cw = FFN_CHUNKS[0]
    for i, r0 in enumerate((0, half)):
        hid_ref[r0:r0 + half, c0:c0 + cw] = swiglu(us[i], c0, cw)
    for c0, cw in FFN_CHUNKS[1:]:
        hid_ref[:, c0:c0 + cw] = swiglu(u, c0, cw)
    out = x + jnp.dot(hid_ref[...], wd_ref[...], preferred_element_type=F32)
    if final:
        out = _rms(out, gf_ref[...])
    o_ref[...] = out
    _run_casts(cast_in, cast_out)


def _mergeffn(h, br, gate, w_branch, w_out, norm_g, w_gate, w_up, w_down, norm_final, cast_views, layer, final):
    m = h.shape[0]
    tm = FFN_TM
    cast_in, cast_out, cast_shapes = _cast_specs(cast_views, layer + 1, lambda i: i)
    est = ((N_BRANCH * BRANCH_W + D_MODEL) * D_MODEL * 2 + 3 * D_MODEL * D_FF * 2 + 4 * tm * D_MODEL * 4
           + 2 * tm * 3 * BRANCH_W * 2 + 2 * tm * GATE_W * 2 + tm * D_FF * 2 + 8 * tm * D_MODEL * 4
           + _cast_bytes(cast_views))
    outs = pl.pallas_call(
        functools.partial(_mergeffn_kernel, final=final, n_cast=len(cast_views)),
        grid=(m // tm,),
        in_specs=[
            pl.BlockSpec((tm, D_MODEL), lambda i: (i, 0)),
            pl.BlockSpec((tm, 3 * BRANCH_W), lambda i: (i, 0)),
            pl.BlockSpec((tm, GATE_W), lambda i: (i, 0)),
            _resident((N_BRANCH * BRANCH_W, D_MODEL), lambda i: (0, 0)),
            _resident((D_MODEL, D_MODEL), lambda i: (0, 0)),
            _resident((None, 1, D_MODEL), lambda i: (layer, 0, 0)),
            _resident((D_MODEL, D_FF), lambda i: (0, 0)),
            _resident((D_MODEL, D_FF), lambda i: (0, 0)),
            _resident((D_FF, D_MODEL), lambda i: (0, 0)),
            _resident((1, D_MODEL), lambda i: (0, 0)),
        ] + cast_in,
        out_specs=[pl.BlockSpec((tm, D_MODEL), lambda i: (i, 0))] + cast_out,
        out_shape=[jax.ShapeDtypeStruct((m, D_MODEL), F32)] + cast_shapes,
        scratch_shapes=[pltpu.VMEM((tm, D_FF), BF16)],
        compiler_params=pltpu.CompilerParams(
            dimension_semantics=("arbitrary",), vmem_limit_bytes=_vmem_limit(est)),
        name="mergeffn",
    )(h, br, gate, w_branch, w_out, norm_g, w_gate, w_up, w_down, norm_final, *cast_views)
    return outs[0], [o.reshape(-1, o.shape[-1]) for o in outs[1:]]


def _rotary_table(seq, rot_dim, head_dim, theta):
    half = rot_dim // 2
    inv = np.power(np.float32(theta), -np.arange(half, dtype=np.float32) / np.float32(half))
    ang = np.arange(seq, dtype=np.float32)[:, None] * inv[None, :]
    cos, sin = np.cos(ang), np.sin(ang)
    zeros = np.zeros((seq, half), np.float32)
    tail0 = np.zeros((seq, head_dim - rot_dim), np.float32)
    tail1 = np.ones((seq, head_dim - rot_dim), np.float32)
    c = np.concatenate([cos, cos, tail1], axis=1)
    nxt = np.concatenate([-sin, zeros, tail0], axis=1)
    prv = np.concatenate([zeros, sin, tail0], axis=1)
    return np.concatenate([c, c, nxt, nxt, prv, prv], axis=1).astype(np.float32)


def _retention_tables():
    log_gamma = np.log1p(-np.exp2(-(5.0 + np.arange(RET_HEADS, dtype=np.float32)))).astype(np.float32)
    idx = np.arange(CHUNK, dtype=np.float32)
    rel = idx[:, None] - idx[None, :]
    dmask = np.where(rel[None] >= 0, np.exp(log_gamma[:, None, None] * np.maximum(rel[None], 0.0)), 0.0)
    zeta = np.exp(log_gamma[:, None] * (CHUNK - 1.0 - idx)[None])
    xi = np.exp(log_gamma[:, None] * (idx + 1.0)[None])
    widen = lambda a: np.repeat(a.T, RET_QK_DIM, axis=1)
    return dmask.astype(np.float32), np.concatenate([widen(xi), widen(zeta)], axis=1).astype(np.float32)


def kernel(x, norm_mix, w_in, attn_sinks, conv_w, w_branch, b_gate, w_out,
           norm_ffn, w_ffn_gate, w_ffn_up, w_ffn_down, norm_final):
    batch, seq, d = x.shape
    m = batch * seq
    assert d == D_MODEL and seq % MIX_T == 0 and m % FFN_TM == 0
    assert w_in.shape == (DEPTH, D_MODEL, D_IN)

    rot_tab = jnp.asarray(np.concatenate([
        _rotary_table(seq, ROPE_DIM, ATTN_HEAD_DIM, ROPE_THETA),
        _rotary_table(seq, RET_QK_DIM, RET_QK_DIM, RET_ROPE_THETA)], axis=1))
    dmask, xz = (jnp.asarray(a) for a in _retention_tables())

    mix_views = [_cast_view(w, m // FFN_TM) for w in (w_in, w_branch, w_out)]
    ffn_views = [_cast_view(w, m // MIX_T) for w in (w_ffn_gate, w_ffn_up, w_ffn_down)]
    mix_w = [w_in[0].astype(BF16), w_branch[0].reshape(N_BRANCH * BRANCH_W, D_MODEL).astype(BF16),
             w_out[0].astype(BF16)]
    norm_mix3 = norm_mix.reshape(DEPTH, 1, D_MODEL)
    norm_ffn3 = norm_ffn.reshape(DEPTH, 1, D_MODEL)
    b_gate3 = b_gate.reshape(DEPTH, 1, GATE_W)
    norm_final2 = norm_final.reshape(1, D_MODEL)

    h = x.reshape(m, D_MODEL)
    for layer in range(DEPTH):
        last = layer == DEPTH - 1
        br, gate, ffn_w = _projmix(h, attn_sinks, norm_mix3, mix_w[0], b_gate3, rot_tab, dmask, xz, conv_w,
                                   ffn_views, batch, seq, layer)
        h, mix_w = _mergeffn(h, br, gate, mix_w[1], mix_w[2], norm_ffn3, *ffn_w, norm_final2,
                             [] if last else mix_views, layer, last)
    return h.reshape(batch, seq, D_MODEL)
```

```python
import functools

import numpy as np
import jax
import jax.numpy as jnp
from jax import lax
from jax.experimental import pallas as pl
from jax.experimental.pallas import tpu as pltpu

F32 = jnp.float32
BF16 = jnp.bfloat16

D_MODEL = 1024
DEPTH = 4
ATTN_Q_HEADS = 8
ATTN_KV_HEADS = 2
ATTN_HEAD_DIM = 64
WINDOW = 128
ROPE_THETA = 500000.0
ROPE_DIM = ATTN_HEAD_DIM // 4
RET_HEADS = 4
RET_QK_DIM = 64
RET_V_DIM = 128
RET_ROPE_THETA = 10000.0
CONV_CH = 512
N_BRANCH = 3
BRANCH_W = 512
D_FF = 2816
EPS = 1e-6
CHUNK = 128

C_AQ, C_AK, C_AV = 0, 512, 640
C_RQ, C_RK, C_RV, C_RG = 768, 1024, 1280, 1792
C_CB, C_CC, C_CX = 2304, 2816, 3328
MIX_W = 3840
GATE_W = N_BRANCH * D_MODEL
D_IN = MIX_W + GATE_W
RET_QK_W = RET_HEADS * RET_QK_DIM

V7X_LANES = 128
V7X_SUBLANES = 8
V7X_BF16_SUBLANES = 16
V7X_VMEM_BYTES = 64 * 1024 * 1024
V7X_SCOPED_VMEM_CAP = 60000 * 1024

ROT_W = 6 * V7X_LANES
XZ_W = 2 * RET_QK_W
CONV_PAD = V7X_SUBLANES

MIX_T = 512
GATE_SLICES = (512,) * 6
MIX_UNITS = (("a", 0), ("r", 0), ("a", 1), ("r", 1), ("a", 2), ("r", 2), ("a", 3), ("r", 3))
MIX_UNIT_SLICES = (1, 0, 1, 0, 1, 1, 1, 1)
MIX_LOOKAHEAD = 1
FFN_TM = 512
FFN_CHUNKS = ((0, 1024), (1024, 1024), (2048, 768))

RET_CHUNK_DECAY = tuple(float(np.exp(np.log1p(-(2.0 ** -(5 + h))) * CHUNK)) for h in range(RET_HEADS))


def _vmem_limit(estimate_bytes):
    return int(min(V7X_SCOPED_VMEM_CAP, max(32 * 1024 * 1024, estimate_bytes * 5 // 4)))


def _rms(x, g):
    ms = jnp.mean(x * x, axis=-1, keepdims=True)
    return x * lax.rsqrt(ms + EPS) * g


def _sigmoid(x):
    return 0.5 * jnp.tanh(0.5 * x) + 0.5


def _resident(block_shape, index_map):
    return pl.BlockSpec(block_shape, index_map, pipeline_mode=pl.Buffered(1))


def _cast_view(w, n_steps):
    depth, n = w.shape[0], w.shape[-1]
    k = int(np.prod(w.shape[1:-1]))
    rows = k // n_steps if k % (n_steps * V7X_BF16_SUBLANES) == 0 else V7X_LANES
    assert k % rows == 0 and k // rows <= n_steps
    return w.reshape(depth, k // rows, rows, n)


def _cast_specs(views, layer, step_of):
    in_specs, out_specs, out_shapes = [], [], []
    for v in views:
        _, nblk, rows, n = v.shape
        blk = lambda *g, nblk=nblk: jnp.minimum(step_of(*g), nblk - 1)
        in_specs.append(pl.BlockSpec((None, None, rows, n), lambda *g, blk=blk: (layer, blk(*g), 0, 0)))
        out_specs.append(pl.BlockSpec((None, rows, n), lambda *g, blk=blk: (blk(*g), 0, 0)))
        out_shapes.append(jax.ShapeDtypeStruct((nblk, rows, n), BF16))
    return in_specs, out_specs, out_shapes


def _cast_bytes(views):
    return sum(2 * v.shape[2] * v.shape[3] * (4 + 2) for v in views)


def _run_casts(cast_in, cast_out):
    for src, dst in zip(cast_in, cast_out):
        dst[...] = src[...].astype(BF16)


def _rot128(z, cos, coef_next, coef_prev, shift):
    return (z * cos + pltpu.roll(z, V7X_LANES - shift, 1) * coef_next
            + pltpu.roll(z, shift, 1) * coef_prev)


class _Attention:
    def __init__(self, p_attn, rot_ref, sinks_ref, br_ref, kx_ref, vx_ref, t, layer):
        T = MIX_T
        half = ATTN_HEAD_DIM
        self.sinks_ref, self.br_ref, self.kx_ref, self.vx_ref, self.layer = sinks_ref, br_ref, kx_ref, vx_ref, layer
        lo_t = lax.broadcasted_iota(jnp.int32, (T, V7X_LANES), 1) < half
        acos, anext, aprev = (rot_ref[:, i * V7X_LANES:(i + 1) * V7X_LANES] for i in range(3))
        scale = ATTN_HEAD_DIM ** -0.5
        self.qb = []
        for g in range(4):
            zg = p_attn[:, C_AQ + g * V7X_LANES:C_AQ + (g + 1) * V7X_LANES]
            self.qb.append((_rot128(zg, acos, anext, aprev, ROPE_DIM // 2) * scale).astype(BF16))
        k = _rot128(p_attn[:, C_AK:C_AK + V7X_LANES], acos, anext, aprev, ROPE_DIM // 2)
        v = p_attn[:, C_AV:C_AV + V7X_LANES]
        for ref, val in ((kx_ref, k), (vx_ref, v)):
            swapped = pltpu.roll(val, half, 1)
            ref[0, CHUNK:CHUNK + T, :] = jnp.where(lo_t, val, 0.0).astype(BF16)
            ref[1, CHUNK:CHUNK + T, :] = jnp.where(lo_t, 0.0, swapped).astype(BF16)
            ref[2, CHUNK:CHUNK + T, :] = jnp.where(lo_t, swapped, 0.0).astype(BF16)
            ref[3, CHUNK:CHUNK + T, :] = jnp.where(lo_t, 0.0, val).astype(BF16)

        qi = lax.broadcasted_iota(jnp.int32, (2 * CHUNK, CHUNK), 0) & (CHUNK - 1)
        kj = lax.broadcasted_iota(jnp.int32, (2 * CHUNK, CHUNK), 1)
        self.from_prev = kj > qi
        self.from_prev_bf = jnp.where(self.from_prev, 1.0, 0.0).astype(BF16)
        self.pad_first = self.from_prev & (kj >= jnp.where(t == 0, 0, CHUNK))
        self.top_rows = lax.broadcasted_iota(jnp.int32, (2 * CHUNK, 1), 0) < CHUNK
        self.lo_2c = lax.broadcasted_iota(jnp.int32, (2 * CHUNK, V7X_LANES), 1) < half

    def scores(self, c):
        r0 = c * CHUNK
        out = []
        for h in range(ATTN_KV_HEADS):
            lhs = jnp.concatenate([self.qb[2 * h][r0:r0 + CHUNK], self.qb[2 * h + 1][r0:r0 + CHUNK]], axis=0)
            for ab in range(2):
                kw = self.kx_ref[2 * h + ab, r0:r0 + 2 * CHUNK, :]
                out.append(lax.dot_general(lhs, kw, (((1,), (1,)), ((), ())), preferred_element_type=F32))
        return out

    def finish(self, c, scores):
        r0 = c * CHUNK
        for h in range(ATTN_KV_HEADS):
            outs = []
            for ab in range(2):
                vw = self.vx_ref[2 * h + ab, r0:r0 + 2 * CHUNK, :]
                s = scores[2 * h + ab]
                s = jnp.where(self.from_prev, s[:, 0:CHUNK], s[:, CHUNK:2 * CHUNK])
                if c == 0:
                    s = jnp.where(self.pad_first, -jnp.inf, s)
                sink = jnp.where(self.top_rows, self.sinks_ref[self.layer, 4 * h + ab],
                                 self.sinks_ref[self.layer, 4 * h + 2 + ab])
                mx = jnp.maximum(jnp.max(s, axis=-1, keepdims=True), sink)
                p = jnp.exp(s - mx)
                den = jnp.sum(p, axis=-1, keepdims=True) + jnp.exp(sink - mx)
                pb = p.astype(BF16)
                p_prev = pb * self.from_prev_bf
                o = jnp.dot(jnp.concatenate([p_prev, pb - p_prev], axis=1), vw, preferred_element_type=F32)
                outs.append(o * (1.0 / den))
            out = jnp.where(self.lo_2c, outs[0], outs[1]).astype(BF16)
            self.br_ref[r0:r0 + CHUNK, (2 * h) * V7X_LANES:(2 * h + 1) * V7X_LANES] = out[0:CHUNK]
            self.br_ref[r0:r0 + CHUNK, (2 * h + 1) * V7X_LANES:(2 * h + 2) * V7X_LANES] = out[CHUNK:2 * CHUNK]

    def carry(self):
        T = MIX_T
        self.kx_ref[:, 0:CHUNK, :] = self.kx_ref[:, T:T + CHUNK, :]
        self.vx_ref[:, 0:CHUNK, :] = self.vx_ref[:, T:T + CHUNK, :]


class _Retention:
    def __init__(self, p_qk, rot_ref, dmask_ref, xz_ref, br_ref, state_ref):
        half = RET_QK_DIM
        self.dmask_ref, self.xz_ref, self.br_ref, self.state_ref = dmask_ref, xz_ref, br_ref, state_ref
        rcos, rnext, rprev = (rot_ref[:, i * V7X_LANES:(i + 1) * V7X_LANES] for i in range(3, 6))
        self.lo_c = lax.broadcasted_iota(jnp.int32, (CHUNK, V7X_LANES), 1) < half
        self.top_c = lax.broadcasted_iota(jnp.int32, (CHUNK, V7X_LANES), 0) < half
        kscale = RET_QK_DIM ** -0.5
        self.qp, self.kp = [], []
        for pp in range(RET_HEADS // 2):
            q0, k0 = pp * V7X_LANES, RET_QK_W + pp * V7X_LANES
            self.qp.append(_rot128(p_qk[:, q0:q0 + V7X_LANES], rcos, rnext, rprev, RET_QK_DIM // 2))
            self.kp.append(_rot128(p_qk[:, k0:k0 + V7X_LANES], rcos, rnext, rprev, RET_QK_DIM // 2) * kscale)

    def scores(self, c):
        r0 = c * CHUNK
        out = []
        for pp in range(RET_HEADS // 2):
            qc = self.qp[pp][r0:r0 + CHUNK]
            kc = self.kp[pp][r0:r0 + CHUNK]
            kbd = jnp.concatenate([jnp.where(self.lo_c, kc, 0.0), jnp.where(self.lo_c, 0.0, kc)],
                                  axis=0).astype(BF16)
            out.append(lax.dot_general(qc.astype(BF16), kbd, (((1,), (1,)), ((), ())),
                                       preferred_element_type=F32))
        return out

    def finish(self, c, scores, p_v, p_g):
        r0 = c * CHUNK
        for pp in range(RET_HEADS // 2):
            qc = self.qp[pp][r0:r0 + CHUNK]
            kc = self.kp[pp][r0:r0 + CHUNK]
            qxi = (qc * self.xz_ref[:, pp * V7X_LANES:(pp + 1) * V7X_LANES]).astype(BF16)
            kz = (kc * self.xz_ref[:, RET_QK_W + pp * V7X_LANES:RET_QK_W + (pp + 1) * V7X_LANES]).astype(BF16)
            for hh in range(2):
                h = 2 * pp + hh
                vh = p_v[r0:r0 + CHUNK, h * V7X_LANES:(h + 1) * V7X_LANES].astype(BF16)
                gh = p_g[r0:r0 + CHUNK, h * V7X_LANES:(h + 1) * V7X_LANES]
                att = (scores[pp][:, hh * CHUNK:(hh + 1) * CHUNK] * self.dmask_ref[h]).astype(BF16)
                st = self.state_ref[h]
                y = jnp.dot(jnp.concatenate([att, qxi], axis=1),
                            jnp.concatenate([vh, st.astype(BF16)], axis=0),
                            preferred_element_type=F32)
                kv = lax.dot_general(kz, vh, (((0,), (0,)), ((), ())), preferred_element_type=F32)
                own_rows = self.top_c if hh == 0 else jnp.logical_not(self.top_c)
                self.state_ref[h] = RET_CHUNK_DECAY[h] * st + jnp.where(own_rows, kv, 0.0)
                yn = y * lax.rsqrt(jnp.mean(y * y, axis=-1, keepdims=True) + EPS)
                out = gh * _sigmoid(gh) * yn
                self.br_ref[r0:r0 + CHUNK, BRANCH_W + h * V7X_LANES:BRANCH_W + (h + 1) * V7X_LANES] = out.astype(BF16)


def _short_conv(cb, cc, cx, convw_ref, br_ref, ubuf_ref):
    T, P = MIX_T, CONV_PAD
    u0 = cc * cx
    for k in (1, 2):
        ubuf_ref[k - 1, P + k:P + k + T, :] = u0
    u1 = ubuf_ref[0, P:P + T, :]
    u2 = ubuf_ref[1, P:P + T, :]
    w = convw_ref[...]
    yc = cb * (w[0:1, :] * u2 + w[1:2, :] * u1 + w[2:3, :] * u0)
    br_ref[:, 2 * BRANCH_W:3 * BRANCH_W] = yc.astype(BF16)
    ubuf_ref[:, P:2 * P, :] = ubuf_ref[:, T + P:T + 2 * P, :]


def _projmix_kernel(*refs, layer, n_cast):
    (sinks_ref, x_ref, g_ref, w_ref, bg_ref, rot_ref, dmask_ref, xz_ref, convw_ref), refs = refs[:9], refs[9:]
    cast_in, refs = refs[:n_cast], refs[n_cast:]
    (br_ref, gate_ref), refs = refs[:2], refs[2:]
    cast_out, (kx_ref, vx_ref, state_ref, ubuf_ref) = refs[:n_cast], refs[n_cast:]
    t = pl.program_id(1)

    @pl.when(t == 0)
    def _():
        kx_ref[:, 0:CHUNK, :] = jnp.zeros((4, CHUNK, V7X_LANES), BF16)
        vx_ref[:, 0:CHUNK, :] = jnp.zeros((4, CHUNK, V7X_LANES), BF16)
        state_ref[...] = jnp.zeros_like(state_ref)
        ubuf_ref[:, CONV_PAD:2 * CONV_PAD, :] = jnp.zeros((2, CONV_PAD, CONV_CH), F32)
        ubuf_ref[:, MIX_T + CONV_PAD:MIX_T + 2 * CONV_PAD, :] = jnp.zeros((2, CONV_PAD, CONV_CH), F32)

    _run_casts(cast_in, cast_out)
    x = x_ref[...]
    xg = x * g_ref[...]
    inv_rms = lax.rsqrt(jnp.mean(x * x, axis=-1, keepdims=True) + EPS)
    u = (xg * inv_rms).astype(BF16)
    proj = lambda c0, c1: jnp.dot(u, w_ref[:, c0:c1], preferred_element_type=F32)
    proj_first = lambda c0, c1: jnp.dot(xg.astype(BF16), w_ref[:, c0:c1], preferred_element_type=F32) * inv_rms
    nch = MIX_T // CHUNK

    attn = _Attention(proj_first(C_AQ, C_RQ), rot_ref, sinks_ref, br_ref, kx_ref, vx_ref, t, layer)
    ret = _Retention(proj(C_RQ, C_RV), rot_ref, dmask_ref, xz_ref, br_ref, state_ref)
    _short_conv(proj(C_CB, C_CC), proj(C_CC, C_CX), proj(C_CX, MIX_W), convw_ref, br_ref, ubuf_ref)
    vals = {"rv": proj(C_RV, C_RG)}

    def gate_slice(c0, width):
        def emit():
            z = proj(MIX_W + c0, MIX_W + c0 + width)
            gate_ref[:, c0:c0 + width] = (z + bg_ref[:, c0:c0 + width]).astype(BF16)
        return emit

    def mix_slice(name, c0, c1):
        def emit():
            vals[name] = proj(c0, c1)
        return emit

    slices = [mix_slice("rg", C_RG, C_CB)]
    c0 = 0
    for width in GATE_SLICES:
        slices.append(gate_slice(c0, width))
        c0 += width
    assert c0 == GATE_W
    units = MIX_UNITS
    assert len(units) == 2 * nch == len(MIX_UNIT_SLICES) and len(slices) >= sum(MIX_UNIT_SLICES)

    def stage1(kind, c):
        return attn.scores(c) if kind == "a" else ret.scores(c)

    def stage2(kind, c, s):
        if kind == "a":
            attn.finish(c, s)
        else:
            ret.finish(c, s, vals["rv"], vals["rg"])

    pending = [stage1(*unit) for unit in units[:MIX_LOOKAHEAD]]
    for i, unit in enumerate(units):
        if i + MIX_LOOKAHEAD < len(units):
            pending.append(stage1(*units[i + MIX_LOOKAHEAD]))
        for _ in range(MIX_UNIT_SLICES[i]):
            slices.pop(0)()
        stage2(*unit, pending.pop(0))
    while slices:
        slices.pop(0)()
    attn.carry()


def _projmix(h, sinks, norm_g, w_in, b_gate, rot_tab, dmask, xz, conv_w, cast_views, batch, seq, layer):
    T = MIX_T
    nt = seq // T
    m = batch * seq
    cast_in, cast_out, cast_shapes = _cast_specs(cast_views, layer, lambda b, t: b * nt + t)
    est = (D_MODEL * D_IN * 2 + 2 * T * D_MODEL * 4 + 2 * T * ROT_W * 4
           + RET_HEADS * CHUNK * CHUNK * 4 + CHUNK * XZ_W * 4 + 2 * T * 3 * BRANCH_W * 2 + 2 * T * GATE_W * 2
           + 2 * 4 * (CHUNK + T) * V7X_LANES * 2 + RET_HEADS * CHUNK * CHUNK * 4 + 2 * (T + 2 * CONV_PAD) * CONV_CH * 4
           + 2 * T * MIX_W * 4 + _cast_bytes(cast_views))
    outs = pl.pallas_call(
        functools.partial(_projmix_kernel, layer=layer, n_cast=len(cast_views)),
        grid=(batch, nt),
        in_specs=[
            pl.BlockSpec(memory_space=pltpu.SMEM),
            pl.BlockSpec((T, D_MODEL), lambda b, t: (b * nt + t, 0)),
            _resident((None, 1, D_MODEL), lambda b, t: (layer, 0, 0)),
            _resident((D_MODEL, D_IN), lambda b, t: (0, 0)),
            _resident((None, 1, GATE_W), lambda b, t: (layer, 0, 0)),
            pl.BlockSpec((T, ROT_W), lambda b, t: (t, 0)),
            _resident((RET_HEADS, CHUNK, CHUNK), lambda b, t: (0, 0, 0)),
            _resident((CHUNK, XZ_W), lambda b, t: (0, 0)),
            _resident((None, 3, CONV_CH), lambda b, t: (layer, 0, 0)),
        ] + cast_in,
        out_specs=[
            pl.BlockSpec((T, 3 * BRANCH_W), lambda b, t: (b * nt + t, 0)),
            pl.BlockSpec((T, GATE_W), lambda b, t: (b * nt + t, 0)),
        ] + cast_out,
        out_shape=[
            jax.ShapeDtypeStruct((m, 3 * BRANCH_W), BF16),
            jax.ShapeDtypeStruct((m, GATE_W), BF16),
        ] + cast_shapes,
        scratch_shapes=[
            pltpu.VMEM((4, CHUNK + T, V7X_LANES), BF16),
            pltpu.VMEM((4, CHUNK + T, V7X_LANES), BF16),
            pltpu.VMEM((RET_HEADS, CHUNK, CHUNK), F32),
            pltpu.VMEM((2, T + 2 * CONV_PAD, CONV_CH), F32),
        ],
        compiler_params=pltpu.CompilerParams(
            dimension_semantics=("arbitrary", "arbitrary"), vmem_limit_bytes=_vmem_limit(est)),
        name="projmix",
    )(sinks, h, norm_g, w_in, b_gate, rot_tab, dmask, xz, conv_w, *cast_views)
    return outs[0], outs[1], [o.reshape(-1, o.shape[-1]) for o in outs[2:]]


def _mergeffn_kernel(*refs, final, n_cast):
    (x_ref, br_ref, gate_ref, wb_ref, wo_ref, g_ref, wg_ref, wu_ref, wd_ref, gf_ref), refs = refs[:10], refs[10:]
    cast_in, refs = refs[:n_cast], refs[n_cast:]
    o_ref, cast_out, hid_ref = refs[0], refs[1:1 + n_cast], refs[1 + n_cast]

    acc = None
    for i in range(N_BRANCH):
        y = jnp.dot(br_ref[:, i * BRANCH_W:(i + 1) * BRANCH_W], wb_ref[i * BRANCH_W:(i + 1) * BRANCH_W, :],
                    preferred_element_type=F32)
        term = _sigmoid(gate_ref[:, i * D_MODEL:(i + 1) * D_MODEL].astype(F32)) * y
        acc = term if acc is None else acc + term
    acc = acc.astype(BF16)
    half = FFN_TM // 2
    xs, us = [], []
    for r0 in (0, half):
        xr = x_ref[r0:r0 + half, :] + jnp.dot(acc[r0:r0 + half], wo_ref[...], preferred_element_type=F32)
        xs.append(xr)
        us.append(_rms(xr, g_ref[...]).astype(BF16))
    x = jnp.concatenate(xs, axis=0)
    u = jnp.concatenate(us, axis=0)

    def swiglu(lhs, c0, cw):
        a = jnp.dot(lhs, wg_ref[:, c0:c0 + cw], preferred_element_type=F32)
        b = jnp.dot(lhs, wu_ref[:, c0:c0 + cw], preferred_element_type=F32)
        return (a * _sigmoid(a) * b).astype(BF16)

    c0, cw = FFN_CHUNKS[0]
    for i, r0 in enumerate((0, half)):
        hid_ref[r0:r0 + half, c0:c0 + cw] = swiglu(us[i], c0, cw)
    for c0, cw in FFN_CHUNKS[1:]:
        hid_ref[:, c0:c0 + cw] = swiglu(u, c0, cw)
    out = x + jnp.dot(hid_ref[...], wd_ref[...], preferred_element_type=F32)
    if final:
        out = _rms(out, gf_ref[...])
    o_ref[...] = out
    _run_casts(cast_in, cast_out)


def _mergeffn(h, br, gate, w_branch, w_out, norm_g, w_gate, w_up, w_down, norm_final, cast_views, layer, final):
    m = h.shape[0]
    tm = FFN_TM
    cast_in, cast_out, cast_shapes = _cast_specs(cast_views, layer + 1, lambda i: i)
    est = ((N_BRANCH * BRANCH_W + D_MODEL) * D_MODEL * 2 + 3 * D_MODEL * D_FF * 2 + 4 * tm * D_MODEL * 4
           + 2 * tm * 3 * BRANCH_W * 2 + 2 * tm * GATE_W * 2 + tm * D_FF * 2 + 8 * tm * D_MODEL * 4
           + _cast_bytes(cast_views))
    outs = pl.pallas_call(
        functools.partial(_mergeffn_kernel, final=final, n_cast=len(cast_views)),
        grid=(m // tm,),
        in_specs=[
            pl.BlockSpec((tm, D_MODEL), lambda i: (i, 0)),
            pl.BlockSpec((tm, 3 * BRANCH_W), lambda i: (i, 0)),
            pl.BlockSpec((tm, GATE_W), lambda i: (i, 0)),
            _resident((N_BRANCH * BRANCH_W, D_MODEL), lambda i: (0, 0)),
            _resident((D_MODEL, D_MODEL), lambda i: (0, 0)),
            _resident((None, 1, D_MODEL), lambda i: (layer, 0, 0)),
            _resident((D_MODEL, D_FF), lambda i: (0, 0)),
            _resident((D_MODEL, D_FF), lambda i: (0, 0)),
            _resident((D_FF, D_MODEL), lambda i: (0, 0)),
            _resident((1, D_MODEL), lambda i: (0, 0)),
        ] + cast_in,
        out_specs=[pl.BlockSpec((tm, D_MODEL), lambda i: (i, 0))] + cast_out,
        out_shape=[jax.ShapeDtypeStruct((m, D_MODEL), F32)] + cast_shapes,
        scratch_shapes=[pltpu.VMEM((tm, D_FF), BF16)],
        compiler_params=pltpu.CompilerParams(
            dimension_semantics=("arbitrary",), vmem_limit_bytes=_vmem_limit(est)),
        name="mergeffn",
    )(h, br, gate, w_branch, w_out, norm_g, w_gate, w_up, w_down, norm_final, *cast_views)
    return outs[0], [o.reshape(-1, o.shape[-1]) for o in outs[1:]]


def _rotary_table(seq, rot_dim, head_dim, theta):
    half = rot_dim // 2
    inv = np.power(np.float32(theta), -np.arange(half, dtype=np.float32) / np.float32(half))
    ang = np.arange(seq, dtype=np.float32)[:, None] * inv[None, :]
    cos, sin = np.cos(ang), np.sin(ang)
    zeros = np.zeros((seq, half), np.float32)
    tail0 = np.zeros((seq, head_dim - rot_dim), np.float32)
    tail1 = np.ones((seq, head_dim - rot_dim), np.float32)
    c = np.concatenate([cos, cos, tail1], axis=1)
    nxt = np.concatenate([-sin, zeros, tail0], axis=1)
    prv = np.concatenate([zeros, sin, tail0], axis=1)
    return np.concatenate([c, c, nxt, nxt, prv, prv], axis=1).astype(np.float32)


def _retention_tables():
    log_gamma = np.log1p(-np.exp2(-(5.0 + np.arange(RET_HEADS, dtype=np.float32)))).astype(np.float32)
    idx = np.arange(CHUNK, dtype=np.float32)
    rel = idx[:, None] - idx[None, :]
    dmask = np.where(rel[None] >= 0, np.exp(log_gamma[:, None, None] * np.maximum(rel[None], 0.0)), 0.0)
    zeta = np.exp(log_gamma[:, None] * (CHUNK - 1.0 - idx)[None])
    xi = np.exp(log_gamma[:, None] * (idx + 1.0)[None])
    widen = lambda a: np.repeat(a.T, RET_QK_DIM, axis=1)
    return dmask.astype(np.float32), np.concatenate([widen(xi), widen(zeta)], axis=1).astype(np.float32)


def kernel(x, norm_mix, w_in, attn_sinks, conv_w, w_branch, b_gate, w_out,
           norm_ffn, w_ffn_gate, w_ffn_up, w_ffn_down, norm_final):
    batch, seq, d = x.shape
    m = batch * seq
    assert d == D_MODEL and seq % MIX_T == 0 and m % FFN_TM == 0
    assert w_in.shape == (DEPTH, D_MODEL, D_IN)

    rot_tab = jnp.asarray(np.concatenate([
        _rotary_table(seq, ROPE_DIM, ATTN_HEAD_DIM, ROPE_THETA),
        _rotary_table(seq, RET_QK_DIM, RET_QK_DIM, RET_ROPE_THETA)], axis=1))
    dmask, xz = (jnp.asarray(a) for a in _retention_tables())

    mix_views = [_cast_view(w, m // FFN_TM) for w in (w_in, w_branch, w_out)]
    ffn_views = [_cast_view(w, m // MIX_T) for w in (w_ffn_gate, w_ffn_up, w_ffn_down)]
    mix_w = [w_in[0].astype(BF16), w_branch[0].reshape(N_BRANCH * BRANCH_W, D_MODEL).astype(BF16),
             w_out[0].astype(BF16)]
    norm_mix3 = norm_mix.reshape(DEPTH, 1, D_MODEL)
    norm_ffn3 = norm_ffn.reshape(DEPTH, 1, D_MODEL)
    b_gate3 = b_gate.reshape(DEPTH, 1, GATE_W)
    norm_final2 = norm_final.reshape(1, D_MODEL)

    h = x.reshape(m, D_MODEL)
    for layer in range(DEPTH):
        last = layer == DEPTH - 1
        br, gate, ffn_w = _projmix(h, attn_sinks, norm_mix3, mix_w[0], b_gate3, rot_tab, dmask, xz, conv_w,
                                   ffn_views, batch, seq, layer)
        h, mix_w = _mergeffn(h, br, gate, mix_w[1], mix_w[2], norm_ffn3, *ffn_w, norm_final2,
                             [] if last else mix_views, layer, last)
    return h.reshape(batch, seq, D_MODEL)
```

```python
import functools

import numpy as np
import jax
import jax.numpy as jnp
from jax import lax
from jax.experimental import pallas as pl
from jax.experimental.pallas import tpu as pltpu

F32 = jnp.float32
BF16 = jnp.bfloat16

D_MODEL = 1024
DEPTH = 4
ATTN_Q_HEADS = 8
ATTN_KV_HEADS = 2
ATTN_HEAD_DIM = 64
WINDOW = 128
ROPE_THETA = 500000.0
ROPE_DIM = ATTN_HEAD_DIM // 4
RET_HEADS = 4
RET_QK_DIM = 64
RET_V_DIM = 128
RET_ROPE_THETA = 10000.0
CONV_CH = 512
N_BRANCH = 3
BRANCH_W = 512
D_FF = 2816
EPS = 1e-6
CHUNK = 128

C_AQ, C_AK, C_AV = 0, 512, 640
C_RQ, C_RK, C_RV, C_RG = 768, 1024, 1280, 1792
C_CB, C_CC, C_CX = 2304, 2816, 3328
MIX_W = 3840
GATE_W = N_BRANCH * D_MODEL
D_IN = MIX_W + GATE_W
RET_QK_W = RET_HEADS * RET_QK_DIM

V7X_LANES = 128
V7X_SUBLANES = 8
V7X_BF16_SUBLANES = 16
V7X_VMEM_BYTES = 64 * 1024 * 1024
V7X_SCOPED_VMEM_CAP = 60000 * 1024

ROT_W = 6 * V7X_LANES
XZ_W = 2 * RET_QK_W
CONV_PAD = V7X_SUBLANES

MIX_T = 512
GATE_SLICES = (512,) * 6
MIX_UNITS = (("a", 0), ("r", 0), ("a", 1), ("r", 1), ("a", 2), ("r", 2), ("a", 3), ("r", 3))
MIX_UNIT_SLICES = (1, 0, 1, 1, 0, 1, 1, 1)
MIX_LOOKAHEAD = 1
FFN_TM = 512
CAST_STEPS = 8
FFN_CHUNKS = ((0, 1024), (1024, 1024), (2048, 768))

RET_CHUNK_DECAY = tuple(float(np.exp(np.log1p(-(2.0 ** -(5 + h))) * CHUNK)) for h in range(RET_HEADS))


def _vmem_limit(estimate_bytes):
    return int(min(V7X_SCOPED_VMEM_CAP, max(32 * 1024 * 1024, estimate_bytes * 5 // 4)))


def _rms(x, g):
    ms = jnp.mean(x * x, axis=-1, keepdims=True)
    return x * lax.rsqrt(ms + EPS) * g


def _sigmoid(x):
    return 0.5 * jnp.tanh(0.5 * x) + 0.5


def _resident(block_shape, index_map):
    return pl.BlockSpec(block_shape, index_map, pipeline_mode=pl.Buffered(1))


def _cast_view(w, n_steps):
    depth, n = w.shape[0], w.shape[-1]
    k = int(np.prod(w.shape[1:-1]))
    rows = k // n_steps if k % (n_steps * V7X_BF16_SUBLANES) == 0 else V7X_LANES
    assert k % rows == 0 and k // rows <= n_steps
    return w.reshape(depth, k // rows, rows, n)


def _cast_specs(views, layer, step_of):
    in_specs, out_specs, out_shapes = [], [], []
    for v in views:
        _, nblk, rows, n = v.shape
        blk = lambda *g, nblk=nblk: jnp.minimum(step_of(*g), nblk - 1)
        in_specs.append(pl.BlockSpec((None, None, rows, n), lambda *g, blk=blk: (layer, blk(*g), 0, 0)))
        out_specs.append(pl.BlockSpec((None, rows, n), lambda *g, blk=blk: (blk(*g), 0, 0)))
        out_shapes.append(jax.ShapeDtypeStruct((nblk, rows, n), BF16))
    return in_specs, out_specs, out_shapes


def _cast_bytes(views):
    return sum(2 * v.shape[2] * v.shape[3] * (4 + 2) for v in views)


def _run_casts(cast_in, cast_out):
    for src, dst in zip(cast_in, cast_out):
        dst[...] = src[...].astype(BF16)


def _cast_kernel(*refs):
    n = len(refs) // 2
    _run_casts(refs[:n], refs[n:])


def _cast_layer(weights, layer):
    views = [_cast_view(w, CAST_STEPS) for w in weights]
    in_specs, out_specs, out_shapes = _cast_specs(views, layer, lambda i: i)
    outs = pl.pallas_call(
        _cast_kernel,
        grid=(CAST_STEPS,),
        in_specs=in_specs,
        out_specs=out_specs,
        out_shape=out_shapes,
        compiler_params=pltpu.CompilerParams(
            dimension_semantics=("arbitrary",), vmem_limit_bytes=_vmem_limit(_cast_bytes(views))),
        name="castw",
    )(*views)
    return [o.reshape(-1, o.shape[-1]) for o in outs]


def _rot128(z, cos, coef_next, coef_prev, shift):
    return (z * cos + pltpu.roll(z, V7X_LANES - shift, 1) * coef_next
            + pltpu.roll(z, shift, 1) * coef_prev)


class _Attention:
    def __init__(self, p_attn, rot_ref, sinks_ref, br_ref, kx_ref, vx_ref, t, layer):
        T = MIX_T
        half = ATTN_HEAD_DIM
        self.sinks_ref, self.br_ref, self.kx_ref, self.vx_ref, self.layer = sinks_ref, br_ref, kx_ref, vx_ref, layer
        lo_t = lax.broadcasted_iota(jnp.int32, (T, V7X_LANES), 1) < half
        acos, anext, aprev = (rot_ref[:, i * V7X_LANES:(i + 1) * V7X_LANES] for i in range(3))
        scale = ATTN_HEAD_DIM ** -0.5
        self.qb = []
        for g in range(4):
            zg = p_attn[:, C_AQ + g * V7X_LANES:C_AQ + (g + 1) * V7X_LANES]
            self.qb.append((_rot128(zg, acos, anext, aprev, ROPE_DIM // 2) * scale).astype(BF16))
        k = _rot128(p_attn[:, C_AK:C_AK + V7X_LANES], acos, anext, aprev, ROPE_DIM // 2)
        v = p_attn[:, C_AV:C_AV + V7X_LANES]
        for ref, val in ((kx_ref, k), (vx_ref, v)):
            swapped = pltpu.roll(val, half, 1)
            ref[0, CHUNK:CHUNK + T, :] = jnp.where(lo_t, val, 0.0).astype(BF16)
            ref[1, CHUNK:CHUNK + T, :] = jnp.where(lo_t, 0.0, swapped).astype(BF16)
            ref[2, CHUNK:CHUNK + T, :] = jnp.where(lo_t, swapped, 0.0).astype(BF16)
            ref[3, CHUNK:CHUNK + T, :] = jnp.where(lo_t, 0.0, val).astype(BF16)

        qi = lax.broadcasted_iota(jnp.int32, (2 * CHUNK, CHUNK), 0) & (CHUNK - 1)
        kj = lax.broadcasted_iota(jnp.int32, (2 * CHUNK, CHUNK), 1)
        self.from_prev = kj > qi
        self.from_prev_bf = jnp.where(self.from_prev, 1.0, 0.0).astype(BF16)
        self.pad_first = self.from_prev & (kj >= jnp.where(t == 0, 0, CHUNK))
        self.top_rows = lax.broadcasted_iota(jnp.int32, (2 * CHUNK, 1), 0) < CHUNK
        self.lo_2c = lax.broadcasted_iota(jnp.int32, (2 * CHUNK, V7X_LANES), 1) < half

    def scores(self, c):
        r0 = c * CHUNK
        out = []
        for h in range(ATTN_KV_HEADS):
            lhs = jnp.concatenate([self.qb[2 * h][r0:r0 + CHUNK], self.qb[2 * h + 1][r0:r0 + CHUNK]], axis=0)
            for ab in range(2):
                kw = self.kx_ref[2 * h + ab, r0:r0 + 2 * CHUNK, :]
                out.append(lax.dot_general(lhs, kw, (((1,), (1,)), ((), ())), preferred_element_type=F32))
        return out

    def finish(self, c, scores):
        r0 = c * CHUNK
        for h in range(ATTN_KV_HEADS):
            outs = []
            for ab in range(2):
                vw = self.vx_ref[2 * h + ab, r0:r0 + 2 * CHUNK, :]
                s = scores[2 * h + ab]
                s = jnp.where(self.from_prev, s[:, 0:CHUNK], s[:, CHUNK:2 * CHUNK])
                if c == 0:
                    s = jnp.where(self.pad_first, -jnp.inf, s)
                sink = jnp.where(self.top_rows, self.sinks_ref[self.layer, 4 * h + ab],
                                 self.sinks_ref[self.layer, 4 * h + 2 + ab])
                mx = jnp.maximum(jnp.max(s, axis=-1, keepdims=True), sink)
                p = jnp.exp(s - mx)
                den = jnp.sum(p, axis=-1, keepdims=True) + jnp.exp(sink - mx)
                pb = p.astype(BF16)
                p_prev = pb * self.from_prev_bf
                o = jnp.dot(jnp.concatenate([p_prev, pb - p_prev], axis=1), vw, preferred_element_type=F32)
                outs.append(o * (1.0 / den))
            out = jnp.where(self.lo_2c, outs[0], outs[1]).astype(BF16)
            self.br_ref[r0:r0 + CHUNK, (2 * h) * V7X_LANES:(2 * h + 1) * V7X_LANES] = out[0:CHUNK]
            self.br_ref[r0:r0 + CHUNK, (2 * h + 1) * V7X_LANES:(2 * h + 2) * V7X_LANES] = out[CHUNK:2 * CHUNK]

    def carry(self):
        T = MIX_T
        self.kx_ref[:, 0:CHUNK, :] = self.kx_ref[:, T:T + CHUNK, :]
        self.vx_ref[:, 0:CHUNK, :] = self.vx_ref[:, T:T + CHUNK, :]


class _Retention:
    def __init__(self, p_qk, rot_ref, dmask_ref, xz_ref, br_ref, state_ref):
        half = RET_QK_DIM
        self.dmask_ref, self.xz_ref, self.br_ref, self.state_ref = dmask_ref, xz_ref, br_ref, state_ref
        rcos, rnext, rprev = (rot_ref[:, i * V7X_LANES:(i + 1) * V7X_LANES] for i in range(3, 6))
        self.lo_c = lax.broadcasted_iota(jnp.int32, (CHUNK, V7X_LANES), 1) < half
        self.top_c = lax.broadcasted_iota(jnp.int32, (CHUNK, V7X_LANES), 0) < half
        kscale = RET_QK_DIM ** -0.5
        self.qp, self.kp = [], []
        for pp in range(RET_HEADS // 2):
            q0, k0 = pp * V7X_LANES, RET_QK_W + pp * V7X_LANES
            self.qp.append(_rot128(p_qk[:, q0:q0 + V7X_LANES], rcos, rnext, rprev, RET_QK_DIM // 2))
            self.kp.append(_rot128(p_qk[:, k0:k0 + V7X_LANES], rcos, rnext, rprev, RET_QK_DIM // 2) * kscale)

    def scores(self, c):
        r0 = c * CHUNK
        out = []
        for pp in range(RET_HEADS // 2):
            qc = self.qp[pp][r0:r0 + CHUNK]
            kc = self.kp[pp][r0:r0 + CHUNK]
            kbd = jnp.concatenate([jnp.where(self.lo_c, kc, 0.0), jnp.where(self.lo_c, 0.0, kc)],
                                  axis=0).astype(BF16)
            out.append(lax.dot_general(qc.astype(BF16), kbd, (((1,), (1,)), ((), ())),
                                       preferred_element_type=F32))
        return out

    def finish(self, c, scores, p_v, p_g):
        r0 = c * CHUNK
        for pp in range(RET_HEADS // 2):
            qc = self.qp[pp][r0:r0 + CHUNK]
            kc = self.kp[pp][r0:r0 + CHUNK]
            qxi = (qc * self.xz_ref[:, pp * V7X_LANES:(pp + 1) * V7X_LANES]).astype(BF16)
            kz = (kc * self.xz_ref[:, RET_QK_W + pp * V7X_LANES:RET_QK_W + (pp + 1) * V7X_LANES]).astype(BF16)
            for hh in range(2):
                h = 2 * pp + hh
                vh = p_v[r0:r0 + CHUNK, h * V7X_LANES:(h + 1) * V7X_LANES].astype(BF16)
                gh = p_g[r0:r0 + CHUNK, h * V7X_LANES:(h + 1) * V7X_LANES]
                att = (scores[pp][:, hh * CHUNK:(hh + 1) * CHUNK] * self.dmask_ref[h]).astype(BF16)
                st = self.state_ref[h]
                y = jnp.dot(jnp.concatenate([att, qxi], axis=1),
                            jnp.concatenate([vh, st.astype(BF16)], axis=0),
                            preferred_element_type=F32)
                kv = lax.dot_general(kz, vh, (((0,), (0,)), ((), ())), preferred_element_type=F32)
                own_rows = self.top_c if hh == 0 else jnp.logical_not(self.top_c)
                self.state_ref[h] = RET_CHUNK_DECAY[h] * st + jnp.where(own_rows, kv, 0.0)
                yn = y * lax.rsqrt(jnp.mean(y * y, axis=-1, keepdims=True) + EPS)
                out = gh * _sigmoid(gh) * yn
                self.br_ref[r0:r0 + CHUNK, BRANCH_W + h * V7X_LANES:BRANCH_W + (h + 1) * V7X_LANES] = out.astype(BF16)


def _short_conv(cb, cc, cx, convw_ref, br_ref, ubuf_ref):
    T, P = MIX_T, CONV_PAD
    u0 = cc * cx
    for k in (1, 2):
        ubuf_ref[k - 1, P + k:P + k + T, :] = u0
    u1 = ubuf_ref[0, P:P + T, :]
    u2 = ubuf_ref[1, P:P + T, :]
    w = convw_ref[...]
    yc = cb * (w[0:1, :] * u2 + w[1:2, :] * u1 + w[2:3, :] * u0)
    br_ref[:, 2 * BRANCH_W:3 * BRANCH_W] = yc.astype(BF16)
    ubuf_ref[:, P:2 * P, :] = ubuf_ref[:, T + P:T + 2 * P, :]


def _projmix_kernel(*refs, layer, n_cast):
    (sinks_ref, x_ref, g_ref, w_ref, bg_ref, rot_ref, dmask_ref, xz_ref, convw_ref), refs = refs[:9], refs[9:]
    cast_in, refs = refs[:n_cast], refs[n_cast:]
    (br_ref, gate_ref), refs = refs[:2], refs[2:]
    cast_out, (kx_ref, vx_ref, state_ref, ubuf_ref) = refs[:n_cast], refs[n_cast:]
    t = pl.program_id(1)

    @pl.when(t == 0)
    def _():
        kx_ref[:, 0:CHUNK, :] = jnp.zeros((4, CHUNK, V7X_LANES), BF16)
        vx_ref[:, 0:CHUNK, :] = jnp.zeros((4, CHUNK, V7X_LANES), BF16)
        state_ref[...] = jnp.zeros_like(state_ref)
        ubuf_ref[:, CONV_PAD:2 * CONV_PAD, :] = jnp.zeros((2, CONV_PAD, CONV_CH), F32)
        ubuf_ref[:, MIX_T + CONV_PAD:MIX_T + 2 * CONV_PAD, :] = jnp.zeros((2, CONV_PAD, CONV_CH), F32)

    _run_casts(cast_in, cast_out)
    x = x_ref[...]
    xg = x * g_ref[...]
    inv_rms = lax.rsqrt(jnp.mean(x * x, axis=-1, keepdims=True) + EPS)
    u = (xg * inv_rms).astype(BF16)
    proj = lambda c0, c1: jnp.dot(u, w_ref[:, c0:c1], preferred_element_type=F32)
    proj_first = lambda c0, c1: jnp.dot(xg.astype(BF16), w_ref[:, c0:c1], preferred_element_type=F32) * inv_rms
    nch = MIX_T // CHUNK

    attn = _Attention(proj_first(C_AQ, C_RQ), rot_ref, sinks_ref, br_ref, kx_ref, vx_ref, t, layer)
    ret = _Retention(proj(C_RQ, C_RV), rot_ref, dmask_ref, xz_ref, br_ref, state_ref)
    _short_conv(proj(C_CB, C_CC), proj(C_CC, C_CX), proj(C_CX, MIX_W), convw_ref, br_ref, ubuf_ref)
    vals = {"rv": proj(C_RV, C_RG)}

    def gate_slice(c0, width):
        def emit():
            z = proj(MIX_W + c0, MIX_W + c0 + width)
            gate_ref[:, c0:c0 + width] = (z + bg_ref[:, c0:c0 + width]).astype(BF16)
        return emit

    def mix_slice(name, c0, c1):
        def emit():
            vals[name] = proj(c0, c1)
        return emit

    slices = [mix_slice("rg", C_RG, C_CB)]
    c0 = 0
    for width in GATE_SLICES:
        slices.append(gate_slice(c0, width))
        c0 += width
    assert c0 == GATE_W
    units = MIX_UNITS
    assert len(units) == 2 * nch == len(MIX_UNIT_SLICES) and len(slices) >= sum(MIX_UNIT_SLICES)

    ret_scores = [ret.scores(c) for c in range(nch)]

    def stage1(kind, c):
        return attn.scores(c) if kind == "a" else ret_scores[c]

    def stage2(kind, c, s):
        if kind == "a":
            attn.finish(c, s)
        else:
            ret.finish(c, s, vals["rv"], vals["rg"])

    pending = [stage1(*unit) for unit in units[:MIX_LOOKAHEAD]]
    for i, unit in enumerate(units):
        if i + MIX_LOOKAHEAD < len(units):
            pending.append(stage1(*units[i + MIX_LOOKAHEAD]))
        for _ in range(MIX_UNIT_SLICES[i]):
            slices.pop(0)()
        stage2(*unit, pending.pop(0))
    while slices:
        slices.pop(0)()
    attn.carry()


def _projmix(h, sinks, norm_g, w_in, b_gate, rot_tab, dmask, xz, conv_w, cast_views, batch, seq, layer):
    T = MIX_T
    nt = seq // T
    m = batch * seq
    cast_in, cast_out, cast_shapes = _cast_specs(cast_views, layer, lambda b, t: b * nt + t)
    est = (D_MODEL * D_IN * 2 + 2 * T * D_MODEL * 4 + 2 * T * ROT_W * 4
           + RET_HEADS * CHUNK * CHUNK * 4 + CHUNK * XZ_W * 4 + 2 * T * 3 * BRANCH_W * 2 + 2 * T * GATE_W * 2
           + 2 * 4 * (CHUNK + T) * V7X_LANES * 2 + RET_HEADS * CHUNK * CHUNK * 4 + 2 * (T + 2 * CONV_PAD) * CONV_CH * 4
           + 2 * T * MIX_W * 4 + _cast_bytes(cast_views))
    outs = pl.pallas_call(
        functools.partial(_projmix_kernel, layer=layer, n_cast=len(cast_views)),
        grid=(batch, nt),
        in_specs=[
            pl.BlockSpec(memory_space=pltpu.SMEM),
            pl.BlockSpec((T, D_MODEL), lambda b, t: (b * nt + t, 0)),
            _resident((None, 1, D_MODEL), lambda b, t: (layer, 0, 0)),
            _resident((D_MODEL, D_IN), lambda b, t: (0, 0)),
            _resident((None, 1, GATE_W), lambda b, t: (layer, 0, 0)),
            pl.BlockSpec((T, ROT_W), lambda b, t: (t, 0)),
            _resident((RET_HEADS, CHUNK, CHUNK), lambda b, t: (0, 0, 0)),
            _resident((CHUNK, XZ_W), lambda b, t: (0, 0)),
            _resident((None, 3, CONV_CH), lambda b, t: (layer, 0, 0)),
        ] + cast_in,
        out_specs=[
            pl.BlockSpec((T, 3 * BRANCH_W), lambda b, t: (b * nt + t, 0)),
            pl.BlockSpec((T, GATE_W), lambda b, t: (b * nt + t, 0)),
        ] + cast_out,
        out_shape=[
            jax.ShapeDtypeStruct((m, 3 * BRANCH_W), BF16),
            jax.ShapeDtypeStruct((m, GATE_W), BF16),
        ] + cast_shapes,
        scratch_shapes=[
            pltpu.VMEM((4, CHUNK + T, V7X_LANES), BF16),
            pltpu.VMEM((4, CHUNK + T, V7X_LANES), BF16),
            pltpu.VMEM((RET_HEADS, CHUNK, CHUNK), F32),
            pltpu.VMEM((2, T + 2 * CONV_PAD, CONV_CH), F32),
        ],
        compiler_params=pltpu.CompilerParams(
            dimension_semantics=("arbitrary", "arbitrary"), vmem_limit_bytes=_vmem_limit(est)),
        name="projmix",
    )(sinks, h, norm_g, w_in, b_gate, rot_tab, dmask, xz, conv_w, *cast_views)
    return outs[0], outs[1], [o.reshape(-1, o.shape[-1]) for o in outs[2:]]


def _mergeffn_kernel(*refs, final, n_cast):
    (x_ref, br_ref, gate_ref, wb_ref, wo_ref, g_ref, wg_ref, wu_ref, wd_ref, gf_ref), refs = refs[:10], refs[10:]
    cast_in, refs = refs[:n_cast], refs[n_cast:]
    o_ref, cast_out, hid_ref = refs[0], refs[1:1 + n_cast], refs[1 + n_cast]

    acc = None
    for i in range(N_BRANCH):
        y = jnp.dot(br_ref[:, i * BRANCH_W:(i + 1) * BRANCH_W], wb_ref[i * BRANCH_W:(i + 1) * BRANCH_W, :],
                    preferred_element_type=F32)
        term = _sigmoid(gate_ref[:, i * D_MODEL:(i + 1) * D_MODEL].astype(F32)) * y
        acc = term if acc is None else acc + term
    acc = acc.astype(BF16)
    half = FFN_TM // 2
    xs, us = [], []
    for r0 in (0, half):
        xr = x_ref[r0:r0 + half, :] + jnp.dot(acc[r0:r0 + half], wo_ref[...], preferred_element_type=F32)
        xs.append(xr)
        us.append(_rms(xr, g_ref[...]).astype(BF16))
    x = jnp.concatenate(xs, axis=0)
    u = jnp.concatenate(us, axis=0)

    def swiglu(lhs, c0, cw):
        a = jnp.dot(lhs, wg_ref[:, c0:c0 + cw], preferred_element_type=F32)
        b = jnp.dot(lhs, wu_ref[:, c0:c0 + cw], preferred_element_type=F32)
        return (a * _sigmoid(a) * b).astype(BF16)

    c0, cw = FFN_CHUNKS[0]
    for i, r0 in enumerate((0, half)):
        hid_ref[r0:r0 + half, c0:c0 + cw] = swiglu(us[i], c0, cw)
    for c0, cw in FFN_CHUNKS[1:]:
        hid_ref[:, c0:c0 + cw] = swiglu(u, c0, cw)
    out = x + jnp.dot(hid_ref[...], wd_ref[...], preferred_element_type=F32)
    if final:
        out = _rms(out, gf_ref[...])
    o_ref[...] = out
    _run_casts(cast_in, cast_out)


def _mergeffn(h, br, gate, w_branch, w_out, norm_g, w_gate, w_up, w_down, norm_final, cast_views, layer, final):
    m = h.shape[0]
    tm = FFN_TM
    cast_in, cast_out, cast_shapes = _cast_specs(cast_views, layer + 1, lambda i: i)
    est = ((N_BRANCH * BRANCH_W + D_MODEL) * D_MODEL * 2 + 3 * D_MODEL * D_FF * 2 + 4 * tm * D_MODEL * 4
           + 2 * tm * 3 * BRANCH_W * 2 + 2 * tm * GATE_W * 2 + tm * D_FF * 2 + 8 * tm * D_MODEL * 4
           + _cast_bytes(cast_views))
    outs = pl.pallas_call(
        functools.partial(_mergeffn_kernel, final=final, n_cast=len(cast_views)),
        grid=(m // tm,),
        in_specs=[
            pl.BlockSpec((tm, D_MODEL), lambda i: (i, 0)),
            pl.BlockSpec((tm, 3 * BRANCH_W), lambda i: (i, 0)),
            pl.BlockSpec((tm, GATE_W), lambda i: (i, 0)),
            _resident((N_BRANCH * BRANCH_W, D_MODEL), lambda i: (0, 0)),
            _resident((D_MODEL, D_MODEL), lambda i: (0, 0)),
            _resident((None, 1, D_MODEL), lambda i: (layer, 0, 0)),
            _resident((D_MODEL, D_FF), lambda i: (0, 0)),
            _resident((D_MODEL, D_FF), lambda i: (0, 0)),
            _resident((D_FF, D_MODEL), lambda i: (0, 0)),
            _resident((1, D_MODEL), lambda i: (0, 0)),
        ] + cast_in,
        out_specs=[pl.BlockSpec((tm, D_MODEL), lambda i: (i, 0))] + cast_out,
        out_shape=[jax.ShapeDtypeStruct((m, D_MODEL), F32)] + cast_shapes,
        scratch_shapes=[pltpu.VMEM((tm, D_FF), BF16)],
        compiler_params=pltpu.CompilerParams(
            dimension_semantics=("arbitrary",), vmem_limit_bytes=_vmem_limit(est)),
        name="mergeffn",
    )(h, br, gate, w_branch, w_out, norm_g, w_gate, w_up, w_down, norm_final, *cast_views)
    return outs[0], [o.reshape(-1, o.shape[-1]) for o in outs[1:]]


def _rotary_table(seq, rot_dim, head_dim, theta):
    half = rot_dim // 2
    inv = np.power(np.float32(theta), -np.arange(half, dtype=np.float32) / np.float32(half))
    ang = np.arange(seq, dtype=np.float32)[:, None] * inv[None, :]
    cos, sin = np.cos(ang), np.sin(ang)
    zeros = np.zeros((seq, half), np.float32)
    tail0 = np.zeros((seq, head_dim - rot_dim), np.float32)
    tail1 = np.ones((seq, head_dim - rot_dim), np.float32)
    c = np.concatenate([cos, cos, tail1], axis=1)
    nxt = np.concatenate([-sin, zeros, tail0], axis=1)
    prv = np.concatenate([zeros, sin, tail0], axis=1)
    return np.concatenate([c, c, nxt, nxt, prv, prv], axis=1).astype(np.float32)


def _retention_tables():
    log_gamma = np.log1p(-np.exp2(-(5.0 + np.arange(RET_HEADS, dtype=np.float32)))).astype(np.float32)
    idx = np.arange(CHUNK, dtype=np.float32)
    rel = idx[:, None] - idx[None, :]
    dmask = np.where(rel[None] >= 0, np.exp(log_gamma[:, None, None] * np.maximum(rel[None], 0.0)), 0.0)
    zeta = np.exp(log_gamma[:, None] * (CHUNK - 1.0 - idx)[None])
    xi = np.exp(log_gamma[:, None] * (idx + 1.0)[None])
    widen = lambda a: np.repeat(a.T, RET_QK_DIM, axis=1)
    return dmask.astype(np.float32), np.concatenate([widen(xi), widen(zeta)], axis=1).astype(np.float32)


def kernel(x, norm_mix, w_in, attn_sinks, conv_w, w_branch, b_gate, w_out,
           norm_ffn, w_ffn_gate, w_ffn_up, w_ffn_down, norm_final):
    batch, seq, d = x.shape
    m = batch * seq
    assert d == D_MODEL and seq % MIX_T == 0 and m % FFN_TM == 0
    assert w_in.shape == (DEPTH, D_MODEL, D_IN)

    rot_tab = jnp.asarray(np.concatenate([
        _rotary_table(seq, ROPE_DIM, ATTN_HEAD_DIM, ROPE_THETA),
        _rotary_table(seq, RET_QK_DIM, RET_QK_DIM, RET_ROPE_THETA)], axis=1))
    dmask, xz = (jnp.asarray(a) for a in _retention_tables())

    mix_views = [_cast_view(w, m // FFN_TM) for w in (w_in, w_branch, w_out)]
    ffn_views = [_cast_view(w, m // MIX_T) for w in (w_ffn_gate, w_ffn_up, w_ffn_down)]
    mix_w = _cast_layer((w_in, w_branch, w_out), 0)
    norm_mix3 = norm_mix.reshape(DEPTH, 1, D_MODEL)
    norm_ffn3 = norm_ffn.reshape(DEPTH, 1, D_MODEL)
    b_gate3 = b_gate.reshape(DEPTH, 1, GATE_W)
    norm_final2 = norm_final.reshape(1, D_MODEL)

    h = x.reshape(m, D_MODEL)
    for layer in range(DEPTH):
        last = layer == DEPTH - 1
        br, gate, ffn_w = _projmix(h, attn_sinks, norm_mix3, mix_w[0], b_gate3, rot_tab, dmask, xz, conv_w,
                                   ffn_views, batch, seq, layer)
        h, mix_w = _mergeffn(h, br, gate, mix_w[1], mix_w[2], norm_ffn3, *ffn_w, norm_final2,
                             [] if last else mix_views, layer, last)
    return h.reshape(batch, seq, D_MODEL)
```

```python
import functools

import numpy as np
import jax
import jax.numpy as jnp
from jax import lax
from jax.experimental import pallas as pl
from jax.experimental.pallas import tpu as pltpu

F32 = jnp.float32
BF16 = jnp.bfloat16

D_MODEL = 1024
DEPTH = 4
ATTN_Q_HEADS = 8
ATTN_KV_HEADS = 2
ATTN_HEAD_DIM = 64
WINDOW = 128
ROPE_THETA = 500000.0
ROPE_DIM = ATTN_HEAD_DIM // 4
RET_HEADS = 4
RET_QK_DIM = 64
RET_V_DIM = 128
RET_ROPE_THETA = 10000.0
CONV_CH = 512
N_BRANCH = 3
BRANCH_W = 512
D_FF = 2816
EPS = 1e-6
CHUNK = 128

C_AQ, C_AK, C_AV = 0, 512, 640
C_RQ, C_RK, C_RV, C_RG = 768, 1024, 1280, 1792
C_CB, C_CC, C_CX = 2304, 2816, 3328
MIX_W = 3840
GATE_W = N_BRANCH * D_MODEL
D_IN = MIX_W + GATE_W
RET_QK_W = RET_HEADS * RET_QK_DIM

V7X_LANES = 128
V7X_SUBLANES = 8
V7X_BF16_SUBLANES = 16
V7X_VMEM_BYTES = 64 * 1024 * 1024
V7X_SCOPED_VMEM_CAP = 60000 * 1024

ROT_W = 6 * V7X_LANES
XZ_W = 2 * RET_QK_W
CONV_PAD = V7X_SUBLANES

MIX_T = 512
GATE_SLICES = (512,) * 6
MIX_UNITS = (("a", 0), ("r", 0), ("a", 1), ("r", 1), ("a", 2), ("r", 2), ("a", 3), ("r", 3))
MIX_UNIT_SLICES = (1, 0, 1, 1, 0, 1, 1, 1)
MIX_LOOKAHEAD = 1
FFN_TM = 512
CAST_STEPS = 8
FFN_CHUNKS = ((0, 1024), (1024, 1024), (2048, 768))

RET_CHUNK_DECAY = tuple(float(np.exp(np.log1p(-(2.0 ** -(5 + h))) * CHUNK)) for h in range(RET_HEADS))


def _vmem_limit(estimate_bytes):
    return int(min(V7X_SCOPED_VMEM_CAP, max(32 * 1024 * 1024, estimate_bytes * 5 // 4)))


def _rms(x, g):
    ms = jnp.mean(x * x, axis=-1, keepdims=True)
    return x * lax.rsqrt(ms + EPS) * g


def _sigmoid(x):
    return 0.5 * jnp.tanh(0.5 * x) + 0.5


def _resident(block_shape, index_map):
    return pl.BlockSpec(block_shape, index_map, pipeline_mode=pl.Buffered(1))


def _cast_view(w, n_steps):
    depth, n = w.shape[0], w.shape[-1]
    k = int(np.prod(w.shape[1:-1]))
    rows = k // n_steps if k % (n_steps * V7X_BF16_SUBLANES) == 0 else V7X_LANES
    assert k % rows == 0 and k // rows <= n_steps
    return w.reshape(depth, k // rows, rows, n)


def _cast_specs(views, layer, step_of):
    in_specs, out_specs, out_shapes = [], [], []
    for v in views:
        _, nblk, rows, n = v.shape
        blk = lambda *g, nblk=nblk: jnp.minimum(step_of(*g), nblk - 1)
        in_specs.append(pl.BlockSpec((None, None, rows, n), lambda *g, blk=blk: (layer, blk(*g), 0, 0)))
        out_specs.append(pl.BlockSpec((None, rows, n), lambda *g, blk=blk: (blk(*g), 0, 0)))
        out_shapes.append(jax.ShapeDtypeStruct((nblk, rows, n), BF16))
    return in_specs, out_specs, out_shapes


def _cast_bytes(views):
    return sum(2 * v.shape[2] * v.shape[3] * (4 + 2) for v in views)


def _run_casts(cast_in, cast_out):
    for src, dst in zip(cast_in, cast_out):
        dst[...] = src[...].astype(BF16)


def _cast_kernel(*refs):
    n = len(refs) // 2
    _run_casts(refs[:n], refs[n:])


def _cast_layer(weights, layer):
    views = [_cast_view(w, CAST_STEPS) for w in weights]
    in_specs, out_specs, out_shapes = _cast_specs(views, layer, lambda i: i)
    outs = pl.pallas_call(
        _cast_kernel,
        grid=(CAST_STEPS,),
        in_specs=in_specs,
        out_specs=out_specs,
        out_shape=out_shapes,
        compiler_params=pltpu.CompilerParams(
            dimension_semantics=("arbitrary",), vmem_limit_bytes=V7X_SCOPED_VMEM_CAP),
        name="castw",
    )(*views)
    return [o.reshape(-1, o.shape[-1]) for o in outs]


def _rot128(z, cos, coef_next, coef_prev, shift):
    return (z * cos + pltpu.roll(z, V7X_LANES - shift, 1) * coef_next
            + pltpu.roll(z, shift, 1) * coef_prev)


class _Attention:
    def __init__(self, p_attn, rot_ref, sinks_ref, br_ref, kx_ref, vx_ref, t, layer):
        T = MIX_T
        half = ATTN_HEAD_DIM
        self.sinks_ref, self.br_ref, self.kx_ref, self.vx_ref, self.layer = sinks_ref, br_ref, kx_ref, vx_ref, layer
        lo_t = lax.broadcasted_iota(jnp.int32, (T, V7X_LANES), 1) < half
        acos, anext, aprev = (rot_ref[:, i * V7X_LANES:(i + 1) * V7X_LANES] for i in range(3))
        scale = ATTN_HEAD_DIM ** -0.5
        self.qb = []
        for g in range(4):
            zg = p_attn[:, C_AQ + g * V7X_LANES:C_AQ + (g + 1) * V7X_LANES]
            self.qb.append((_rot128(zg, acos, anext, aprev, ROPE_DIM // 2) * scale).astype(BF16))
        k = _rot128(p_attn[:, C_AK:C_AK + V7X_LANES], acos, anext, aprev, ROPE_DIM // 2)
        v = p_attn[:, C_AV:C_AV + V7X_LANES]
        for ref, val in ((kx_ref, k), (vx_ref, v)):
            swapped = pltpu.roll(val, half, 1)
            ref[0, CHUNK:CHUNK + T, :] = jnp.where(lo_t, val, 0.0).astype(BF16)
            ref[1, CHUNK:CHUNK + T, :] = jnp.where(lo_t, 0.0, swapped).astype(BF16)
            ref[2, CHUNK:CHUNK + T, :] = jnp.where(lo_t, swapped, 0.0).astype(BF16)
            ref[3, CHUNK:CHUNK + T, :] = jnp.where(lo_t, 0.0, val).astype(BF16)

        qi = lax.broadcasted_iota(jnp.int32, (2 * CHUNK, CHUNK), 0) & (CHUNK - 1)
        kj = lax.broadcasted_iota(jnp.int32, (2 * CHUNK, CHUNK), 1)
        self.from_prev = kj > qi
        self.from_prev_bf = jnp.where(self.from_prev, 1.0, 0.0).astype(BF16)
        self.pad_first = self.from_prev & (kj >= jnp.where(t == 0, 0, CHUNK))
        self.top_rows = lax.broadcasted_iota(jnp.int32, (2 * CHUNK, 1), 0) < CHUNK
        self.lo_2c = lax.broadcasted_iota(jnp.int32, (2 * CHUNK, V7X_LANES), 1) < half

    def scores(self, c):
        r0 = c * CHUNK
        out = []
        for h in range(ATTN_KV_HEADS):
            lhs = jnp.concatenate([self.qb[2 * h][r0:r0 + CHUNK], self.qb[2 * h + 1][r0:r0 + CHUNK]], axis=0)
            for ab in range(2):
                kw = self.kx_ref[2 * h + ab, r0:r0 + 2 * CHUNK, :]
                out.append(lax.dot_general(lhs, kw, (((1,), (1,)), ((), ())), preferred_element_type=F32))
        return out

    def finish(self, c, scores):
        r0 = c * CHUNK
        for h in range(ATTN_KV_HEADS):
            outs = []
            for ab in range(2):
                vw = self.vx_ref[2 * h + ab, r0:r0 + 2 * CHUNK, :]
                s = scores[2 * h + ab]
                s = jnp.where(self.from_prev, s[:, 0:CHUNK], s[:, CHUNK:2 * CHUNK])
                if c == 0:
                    s = jnp.where(self.pad_first, -jnp.inf, s)
                sink = jnp.where(self.top_rows, self.sinks_ref[self.layer, 4 * h + ab],
                                 self.sinks_ref[self.layer, 4 * h + 2 + ab])
                mx = jnp.maximum(jnp.max(s, axis=-1, keepdims=True), sink)
                p = jnp.exp(s - mx)
                den = jnp.sum(p, axis=-1, keepdims=True) + jnp.exp(sink - mx)
                pb = p.astype(BF16)
                p_prev = pb * self.from_prev_bf
                o = jnp.dot(jnp.concatenate([p_prev, pb - p_prev], axis=1), vw, preferred_element_type=F32)
                outs.append(o * (1.0 / den))
            out = jnp.where(self.lo_2c, outs[0], outs[1]).astype(BF16)
            self.br_ref[r0:r0 + CHUNK, (2 * h) * V7X_LANES:(2 * h + 1) * V7X_LANES] = out[0:CHUNK]
            self.br_ref[r0:r0 + CHUNK, (2 * h + 1) * V7X_LANES:(2 * h + 2) * V7X_LANES] = out[CHUNK:2 * CHUNK]

    def carry(self):
        T = MIX_T
        self.kx_ref[:, 0:CHUNK, :] = self.kx_ref[:, T:T + CHUNK, :]
        self.vx_ref[:, 0:CHUNK, :] = self.vx_ref[:, T:T + CHUNK, :]


class _Retention:
    def __init__(self, p_qk, rot_ref, dmask_ref, xz_ref, br_ref, state_ref):
        half = RET_QK_DIM
        self.dmask_ref, self.xz_ref, self.br_ref, self.state_ref = dmask_ref, xz_ref, br_ref, state_ref
        rcos, rnext, rprev = (rot_ref[:, i * V7X_LANES:(i + 1) * V7X_LANES] for i in range(3, 6))
        self.lo_c = lax.broadcasted_iota(jnp.int32, (CHUNK, V7X_LANES), 1) < half
        self.top_c = lax.broadcasted_iota(jnp.int32, (CHUNK, V7X_LANES), 0) < half
        kscale = RET_QK_DIM ** -0.5
        self.qp, self.kp = [], []
        for pp in range(RET_HEADS // 2):
            q0, k0 = pp * V7X_LANES, RET_QK_W + pp * V7X_LANES
            self.qp.append(_rot128(p_qk[:, q0:q0 + V7X_LANES], rcos, rnext, rprev, RET_QK_DIM // 2))
            self.kp.append(_rot128(p_qk[:, k0:k0 + V7X_LANES], rcos, rnext, rprev, RET_QK_DIM // 2) * kscale)

    def scores(self, c):
        r0 = c * CHUNK
        out = []
        for pp in range(RET_HEADS // 2):
            qc = self.qp[pp][r0:r0 + CHUNK]
            kc = self.kp[pp][r0:r0 + CHUNK]
            kbd = jnp.concatenate([jnp.where(self.lo_c, kc, 0.0), jnp.where(self.lo_c, 0.0, kc)],
                                  axis=0).astype(BF16)
            out.append(lax.dot_general(qc.astype(BF16), kbd, (((1,), (1,)), ((), ())),
                                       preferred_element_type=F32))
        return out

    def finish(self, c, scores, p_v, p_g):
        r0 = c * CHUNK
        for pp in range(RET_HEADS // 2):
            qc = self.qp[pp][r0:r0 + CHUNK]
            kc = self.kp[pp][r0:r0 + CHUNK]
            qxi = (qc * self.xz_ref[:, pp * V7X_LANES:(pp + 1) * V7X_LANES]).astype(BF16)
            kz = (kc * self.xz_ref[:, RET_QK_W + pp * V7X_LANES:RET_QK_W + (pp + 1) * V7X_LANES]).astype(BF16)
            for hh in range(2):
                h = 2 * pp + hh
                vh = p_v[r0:r0 + CHUNK, h * V7X_LANES:(h + 1) * V7X_LANES].astype(BF16)
                gh = p_g[r0:r0 + CHUNK, h * V7X_LANES:(h + 1) * V7X_LANES]
                att = (scores[pp][:, hh * CHUNK:(hh + 1) * CHUNK] * self.dmask_ref[h]).astype(BF16)
                st = self.state_ref[h]
                y = jnp.dot(jnp.concatenate([att, qxi], axis=1),
                            jnp.concatenate([vh, st.astype(BF16)], axis=0),
                            preferred_element_type=F32)
                kv = lax.dot_general(kz, vh, (((0,), (0,)), ((), ())), preferred_element_type=F32)
                own_rows = self.top_c if hh == 0 else jnp.logical_not(self.top_c)
                self.state_ref[h] = RET_CHUNK_DECAY[h] * st + jnp.where(own_rows, kv, 0.0)
                yn = y * lax.rsqrt(jnp.mean(y * y, axis=-1, keepdims=True) + EPS)
                out = gh * _sigmoid(gh) * yn
                self.br_ref[r0:r0 + CHUNK, BRANCH_W + h * V7X_LANES:BRANCH_W + (h + 1) * V7X_LANES] = out.astype(BF16)


def _short_conv(cb, cc, cx, convw_ref, br_ref, ubuf_ref):
    T, P = MIX_T, CONV_PAD
    u0 = cc * cx
    for k in (1, 2):
        ubuf_ref[k - 1, P + k:P + k + T, :] = u0
    u1 = ubuf_ref[0, P:P + T, :]
    u2 = ubuf_ref[1, P:P + T, :]
    w = convw_ref[...]
    yc = cb * (w[0:1, :] * u2 + w[1:2, :] * u1 + w[2:3, :] * u0)
    br_ref[:, 2 * BRANCH_W:3 * BRANCH_W] = yc.astype(BF16)
    ubuf_ref[:, P:2 * P, :] = ubuf_ref[:, T + P:T + 2 * P, :]


def _projmix_kernel(*refs, layer, n_cast):
    (sinks_ref, x_ref, g_ref, w_ref, bg_ref, rot_ref, dmask_ref, xz_ref, convw_ref), refs = refs[:9], refs[9:]
    cast_in, refs = refs[:n_cast], refs[n_cast:]
    (br_ref, gate_ref), refs = refs[:2], refs[2:]
    cast_out, (kx_ref, vx_ref, state_ref, ubuf_ref) = refs[:n_cast], refs[n_cast:]
    t = pl.program_id(1)

    @pl.when(t == 0)
    def _():
        kx_ref[:, 0:CHUNK, :] = jnp.zeros((4, CHUNK, V7X_LANES), BF16)
        vx_ref[:, 0:CHUNK, :] = jnp.zeros((4, CHUNK, V7X_LANES), BF16)
        state_ref[...] = jnp.zeros_like(state_ref)
        ubuf_ref[:, CONV_PAD:2 * CONV_PAD, :] = jnp.zeros((2, CONV_PAD, CONV_CH), F32)
        ubuf_ref[:, MIX_T + CONV_PAD:MIX_T + 2 * CONV_PAD, :] = jnp.zeros((2, CONV_PAD, CONV_CH), F32)

    _run_casts(cast_in, cast_out)
    x = x_ref[...]
    xg = x * g_ref[layer:layer + 1, :]
    inv_rms = lax.rsqrt(jnp.mean(x * x, axis=-1, keepdims=True) + EPS)
    u = (xg * inv_rms).astype(BF16)
    proj = lambda c0, c1: jnp.dot(u, w_ref[:, c0:c1], preferred_element_type=F32)
    proj_first = lambda c0, c1: jnp.dot(xg.astype(BF16), w_ref[:, c0:c1], preferred_element_type=F32) * inv_rms
    nch = MIX_T // CHUNK

    attn = _Attention(proj_first(C_AQ, C_RQ), rot_ref, sinks_ref, br_ref, kx_ref, vx_ref, t, layer)
    ret = _Retention(proj(C_RQ, C_RV), rot_ref, dmask_ref, xz_ref, br_ref, state_ref)
    _short_conv(proj(C_CB, C_CC), proj(C_CC, C_CX), proj(C_CX, MIX_W), convw_ref, br_ref, ubuf_ref)
    vals = {"rv": proj(C_RV, C_RG)}

    def gate_slice(c0, width):
        def emit():
            z = proj(MIX_W + c0, MIX_W + c0 + width)
            gate_ref[:, c0:c0 + width] = (z + bg_ref[:, c0:c0 + width]).astype(BF16)
        return emit

    def mix_slice(name, c0, c1):
        def emit():
            vals[name] = proj(c0, c1)
        return emit

    slices = [mix_slice("rg", C_RG, C_CB)]
    c0 = 0
    for width in GATE_SLICES:
        slices.append(gate_slice(c0, width))
        c0 += width
    assert c0 == GATE_W
    units = MIX_UNITS
    assert len(units) == 2 * nch == len(MIX_UNIT_SLICES) and len(slices) >= sum(MIX_UNIT_SLICES)

    ret_scores = [ret.scores(c) for c in range(nch)]

    def stage1(kind, c):
        return attn.scores(c) if kind == "a" else ret_scores[c]

    def stage2(kind, c, s):
        if kind == "a":
            attn.finish(c, s)
        else:
            ret.finish(c, s, vals["rv"], vals["rg"])

    pending = [stage1(*unit) for unit in units[:MIX_LOOKAHEAD]]
    for i, unit in enumerate(units):
        if i + MIX_LOOKAHEAD < len(units):
            pending.append(stage1(*units[i + MIX_LOOKAHEAD]))
        for _ in range(MIX_UNIT_SLICES[i]):
            slices.pop(0)()
        stage2(*unit, pending.pop(0))
    while slices:
        slices.pop(0)()
    attn.carry()


def _projmix(h, sinks, norm_g, w_in, b_gate, rot_tab, dmask, xz, conv_w, cast_views, batch, seq, layer):
    T = MIX_T
    nt = seq // T
    m = batch * seq
    cast_in, cast_out, cast_shapes = _cast_specs(cast_views, layer, lambda b, t: b * nt + t)
    est = (D_MODEL * D_IN * 2 + 2 * T * D_MODEL * 4 + 2 * T * ROT_W * 4
           + RET_HEADS * CHUNK * CHUNK * 4 + CHUNK * XZ_W * 4 + 2 * T * 3 * BRANCH_W * 2 + 2 * T * GATE_W * 2
           + 2 * 4 * (CHUNK + T) * V7X_LANES * 2 + RET_HEADS * CHUNK * CHUNK * 4 + 2 * (T + 2 * CONV_PAD) * CONV_CH * 4
           + 2 * T * MIX_W * 4 + _cast_bytes(cast_views))
    outs = pl.pallas_call(
        functools.partial(_projmix_kernel, layer=layer, n_cast=len(cast_views)),
        grid=(batch, nt),
        in_specs=[
            pl.BlockSpec(memory_space=pltpu.SMEM),
            pl.BlockSpec((T, D_MODEL), lambda b, t: (b * nt + t, 0)),
            _resident((DEPTH, D_MODEL), lambda b, t: (0, 0)),
            _resident((D_MODEL, D_IN), lambda b, t: (0, 0)),
            _resident((None, 1, GATE_W), lambda b, t: (layer, 0, 0)),
            pl.BlockSpec((T, ROT_W), lambda b, t: (t, 0)),
            _resident((RET_HEADS, CHUNK, CHUNK), lambda b, t: (0, 0, 0)),
            _resident((CHUNK, XZ_W), lambda b, t: (0, 0)),
            _resident((None, 3, CONV_CH), lambda b, t: (layer, 0, 0)),
        ] + cast_in,
        out_specs=[
            pl.BlockSpec((T, 3 * BRANCH_W), lambda b, t: (b * nt + t, 0)),
            pl.BlockSpec((T, GATE_W), lambda b, t: (b * nt + t, 0)),
        ] + cast_out,
        out_shape=[
            jax.ShapeDtypeStruct((m, 3 * BRANCH_W), BF16),
            jax.ShapeDtypeStruct((m, GATE_W), BF16),
        ] + cast_shapes,
        scratch_shapes=[
            pltpu.VMEM((4, CHUNK + T, V7X_LANES), BF16),
            pltpu.VMEM((4, CHUNK + T, V7X_LANES), BF16),
            pltpu.VMEM((RET_HEADS, CHUNK, CHUNK), F32),
            pltpu.VMEM((2, T + 2 * CONV_PAD, CONV_CH), F32),
        ],
        compiler_params=pltpu.CompilerParams(
            dimension_semantics=("arbitrary", "arbitrary"), vmem_limit_bytes=_vmem_limit(est)),
        name="projmix",
    )(sinks, h, norm_g, w_in, b_gate, rot_tab, dmask, xz, conv_w, *cast_views)
    return outs[0], outs[1], [o.reshape(-1, o.shape[-1]) for o in outs[2:]]


def _mergeffn_kernel(*refs, layer, final, n_cast):
    (x_ref, br_ref, gate_ref, wb_ref, wo_ref, g_ref, wg_ref, wu_ref, wd_ref, gf_ref), refs = refs[:10], refs[10:]
    cast_in, refs = refs[:n_cast], refs[n_cast:]
    o_ref, cast_out, hid_ref = refs[0], refs[1:1 + n_cast], refs[1 + n_cast]

    acc = None
    for i in range(N_BRANCH):
        y = jnp.dot(br_ref[:, i * BRANCH_W:(i + 1) * BRANCH_W], wb_ref[i * BRANCH_W:(i + 1) * BRANCH_W, :],
                    preferred_element_type=F32)
        term = _sigmoid(gate_ref[:, i * D_MODEL:(i + 1) * D_MODEL].astype(F32)) * y
        acc = term if acc is None else acc + term
    acc = acc.astype(BF16)
    half = FFN_TM // 2
    xs, us = [], []
    for r0 in (0, half):
        xr = x_ref[r0:r0 + half, :] + jnp.dot(acc[r0:r0 + half], wo_ref[...], preferred_element_type=F32)
        xs.append(xr)
        us.append(_rms(xr, g_ref[layer:layer + 1, :]).astype(BF16))
    x = jnp.concatenate(xs, axis=0)
    u = jnp.concatenate(us, axis=0)

    def swiglu(lhs, c0, cw):
        a = jnp.dot(lhs, wg_ref[:, c0:c0 + cw], preferred_element_type=F32)
        b = jnp.dot(lhs, wu_ref[:, c0:c0 + cw], preferred_element_type=F32)
        return (a * _sigmoid(a) * b).astype(BF16)

    c0, cw = FFN_CHUNKS[0]
    for i, r0 in enumerate((0, half)):
        hid_ref[r0:r0 + half, c0:c0 + cw] = swiglu(us[i], c0, cw)
    for c0, cw in FFN_CHUNKS[1:]:
        hid_ref[:, c0:c0 + cw] = swiglu(u, c0, cw)
    out = x + jnp.dot(hid_ref[...], wd_ref[...], preferred_element_type=F32)
    if final:
        out = _rms(out, gf_ref[...])
    o_ref[...] = out
    _run_casts(cast_in, cast_out)


def _mergeffn(h, br, gate, w_branch, w_out, norm_g, w_gate, w_up, w_down, norm_final, cast_views, layer, final):
    m = h.shape[0]
    tm = FFN_TM
    cast_in, cast_out, cast_shapes = _cast_specs(cast_views, layer + 1, lambda i: i)
    est = ((N_BRANCH * BRANCH_W + D_MODEL) * D_MODEL * 2 + 3 * D_MODEL * D_FF * 2 + 4 * tm * D_MODEL * 4
           + 2 * tm * 3 * BRANCH_W * 2 + 2 * tm * GATE_W * 2 + tm * D_FF * 2 + 8 * tm * D_MODEL * 4
           + _cast_bytes(cast_views))
    outs = pl.pallas_call(
        functools.partial(_mergeffn_kernel, layer=layer, final=final, n_cast=len(cast_views)),
        grid=(m // tm,),
        in_specs=[
            pl.BlockSpec((tm, D_MODEL), lambda i: (i, 0)),
            pl.BlockSpec((tm, 3 * BRANCH_W), lambda i: (i, 0)),
            pl.BlockSpec((tm, GATE_W), lambda i: (i, 0)),
            _resident((N_BRANCH * BRANCH_W, D_MODEL), lambda i: (0, 0)),
            _resident((D_MODEL, D_MODEL), lambda i: (0, 0)),
            _resident((DEPTH, D_MODEL), lambda i: (0, 0)),
            _resident((D_MODEL, D_FF), lambda i: (0, 0)),
            _resident((D_MODEL, D_FF), lambda i: (0, 0)),
            _resident((D_FF, D_MODEL), lambda i: (0, 0)),
            _resident((1, D_MODEL), lambda i: (0, 0)),
        ] + cast_in,
        out_specs=[pl.BlockSpec((tm, D_MODEL), lambda i: (i, 0))] + cast_out,
        out_shape=[jax.ShapeDtypeStruct((m, D_MODEL), F32)] + cast_shapes,
        scratch_shapes=[pltpu.VMEM((tm, D_FF), BF16)],
        compiler_params=pltpu.CompilerParams(
            dimension_semantics=("arbitrary",), vmem_limit_bytes=_vmem_limit(est)),
        name="mergeffn",
    )(h, br, gate, w_branch, w_out, norm_g, w_gate, w_up, w_down, norm_final, *cast_views)
    return outs[0], [o.reshape(-1, o.shape[-1]) for o in outs[1:]]


def _rotary_table(seq, rot_dim, head_dim, theta):
    half = rot_dim // 2
    inv = np.power(np.float32(theta), -np.arange(half, dtype=np.float32) / np.float32(half))
    ang = np.arange(seq, dtype=np.float32)[:, None] * inv[None, :]
    cos, sin = np.cos(ang), np.sin(ang)
    zeros = np.zeros((seq, half), np.float32)
    tail0 = np.zeros((seq, head_dim - rot_dim), np.float32)
    tail1 = np.ones((seq, head_dim - rot_dim), np.float32)
    c = np.concatenate([cos, cos, tail1], axis=1)
    nxt = np.concatenate([-sin, zeros, tail0], axis=1)
    prv = np.concatenate([zeros, sin, tail0], axis=1)
    return np.concatenate([c, c, nxt, nxt, prv, prv], axis=1).astype(np.float32)


def _retention_tables():
    log_gamma = np.log1p(-np.exp2(-(5.0 + np.arange(RET_HEADS, dtype=np.float32)))).astype(np.float32)
    idx = np.arange(CHUNK, dtype=np.float32)
    rel = idx[:, None] - idx[None, :]
    dmask = np.where(rel[None] >= 0, np.exp(log_gamma[:, None, None] * np.maximum(rel[None], 0.0)), 0.0)
    zeta = np.exp(log_gamma[:, None] * (CHUNK - 1.0 - idx)[None])
    xi = np.exp(log_gamma[:, None] * (idx + 1.0)[None])
    widen = lambda a: np.repeat(a.T, RET_QK_DIM, axis=1)
    return dmask.astype(np.float32), np.concatenate([widen(xi), widen(zeta)], axis=1).astype(np.float32)


def kernel(x, norm_mix, w_in, attn_sinks, conv_w, w_branch, b_gate, w_out,
           norm_ffn, w_ffn_gate, w_ffn_up, w_ffn_down, norm_final):
    batch, seq, d = x.shape
    m = batch * seq
    assert d == D_MODEL and seq % MIX_T == 0 and m % FFN_TM == 0
    assert w_in.shape == (DEPTH, D_MODEL, D_IN)

    rot_tab = jnp.asarray(np.concatenate([
        _rotary_table(seq, ROPE_DIM, ATTN_HEAD_DIM, ROPE_THETA),
        _rotary_table(seq, RET_QK_DIM, RET_QK_DIM, RET_ROPE_THETA)], axis=1))
    dmask, xz = (jnp.asarray(a) for a in _retention_tables())

    mix_views = [_cast_view(w, m // FFN_TM) for w in (w_in, w_branch, w_out)]
    ffn_views = [_cast_view(w, m // MIX_T) for w in (w_ffn_gate, w_ffn_up, w_ffn_down)]
    mix_w = _cast_layer((w_in, w_branch, w_out), 0)
    norm_mix3, norm_ffn3 = norm_mix, norm_ffn
    b_gate3 = b_gate.reshape(DEPTH, 1, GATE_W)
    norm_final2 = norm_final.reshape(1, D_MODEL)

    h = x.reshape(m, D_MODEL)
    for layer in range(DEPTH):
        last = layer == DEPTH - 1
        br, gate, ffn_w = _projmix(h, attn_sinks, norm_mix3, mix_w[0], b_gate3, rot_tab, dmask, xz, conv_w,
                                   ffn_views, batch, seq, layer)
        h, mix_w = _mergeffn(h, br, gate, mix_w[1], mix_w[2], norm_ffn3, *ffn_w, norm_final2,
                             [] if last else mix_views, layer, last)
    return h.reshape(batch, seq, D_MODEL)
```
